```python
import jax, jax.numpy as jnp
from jax import lax
import numpy as np

D_MODEL = 4096
BATCH = 2
SEQ = 8192
DEPTH = 4

PLE_DIM = 256
EPS = 1e-6
CONV_DIM = 2048
CONV_WIDTH = 31
GLA_HEADS = 8
GLA_DK = 128
GLA_DV = 256
GLA_K = GLA_HEADS * GLA_DK
GLA_V = GLA_HEADS * GLA_DV
GLA_GATE_RANK = 16
GLA_TAU = 16.0
GLA_CHUNK = 64
N_BRANCH = 2
D_FF_DENSE = 5632
N_EXPERTS = 8
TOP_K = 2
D_FF_EXPERT = 1024
N_DENSE = (DEPTH + 1) // 2
N_MOE = DEPTH // 2
IN_SPLITS = (2 * CONV_DIM, GLA_K, GLA_K, GLA_V, GLA_V, GLA_GATE_RANK, N_BRANCH * D_MODEL)
IN_COLS = 2 * CONV_DIM + 2 * GLA_K + 2 * GLA_V + GLA_GATE_RANK + N_BRANCH * D_MODEL

kernel_name = "hybrid_conformer_gla_moe_ple"


def rmsnorm(x, g):
    xf = x.astype(jnp.float32)
    y = xf * lax.rsqrt(jnp.mean(xf * xf, axis=-1, keepdims=True) + EPS)
    return (y * g.astype(jnp.float32)).astype(x.dtype)


def layernorm(x, g, b):
    xf = x.astype(jnp.float32)
    mu = jnp.mean(xf, axis=-1, keepdims=True)
    var = jnp.mean(jnp.square(xf - mu), axis=-1, keepdims=True)
    y = (xf - mu) * lax.rsqrt(var + EPS)
    return (y * g.astype(jnp.float32) + b.astype(jnp.float32)).astype(x.dtype)


def split_in(z):
    offs = [int(o) for o in np.cumsum(IN_SPLITS)[:-1]]
    return jnp.split(z, offs, axis=-1)


def conformer_conv(u_glu, w_dw, b_dw, ln_g, ln_b, w_out):
    a, gte = jnp.split(u_glu, 2, axis=-1)
    u = a * jax.nn.sigmoid(gte)
    u = lax.conv_general_dilated(
        u, w_dw[:, None, :], window_strides=(1,), padding=[(CONV_WIDTH - 1, 0)],
        dimension_numbers=('NWC', 'WIO', 'NWC'), feature_group_count=CONV_DIM) + b_dw
    u = jax.nn.silu(layernorm(u, ln_g, ln_b))
    return u @ w_out


def gla_chunk_scan(q, k, v, log_a):
    b_, s_, h_, dk = q.shape
    dv = v.shape[-1]
    n = s_ // GLA_CHUNK

    def to_chunks(t):
        return t.reshape(b_, n, GLA_CHUNK, h_, t.shape[-1]).transpose(1, 0, 3, 2, 4)

    qc, kc, vc, ac = to_chunks(q), to_chunks(k), to_chunks(v), to_chunks(log_a)
    causal = jnp.tril(jnp.ones((GLA_CHUNK, GLA_CHUNK), dtype=bool))

    def step(state, inp):
        qi, ki, vi, ai = inp
        bcum = jnp.cumsum(ai, axis=2)
        blast = bcum[:, :, -1:, :]
        diff = bcum[:, :, :, None, :] - bcum[:, :, None, :, :]
        decay = jnp.exp(jnp.where(causal[None, None, :, :, None], diff, -jnp.inf))
        scores = jnp.einsum('bhid,bhijd->bhij', qi, decay * ki[:, :, None, :, :])
        o_intra = jnp.einsum('bhij,bhjv->bhiv', scores, vi)
        o_inter = jnp.einsum('bhid,bhdv->bhiv', qi * jnp.exp(bcum), state)
        k_dec = ki * jnp.exp(blast - bcum)
        new_state = jnp.exp(blast[:, :, 0, :])[..., None] * state + jnp.einsum('bhjd,bhjv->bhdv', k_dec, vi)
        return new_state, o_intra + o_inter

    s0 = jnp.zeros((b_, h_, dk, dv), jnp.float32)
    _, o = lax.scan(step, s0, (qc, kc, vc, ac))
    return o.transpose(1, 0, 3, 2, 4).reshape(b_, s_, h_, dv)


def gla_branch(q, k, v, og, a_lr, w_a_up, b_a, onorm_g, w_out):
    b_, s_, _ = q.shape
    dt = q.dtype
    qh = q.astype(jnp.float32).reshape(b_, s_, GLA_HEADS, GLA_DK) * (GLA_DK ** -0.5)
    kh = k.astype(jnp.float32).reshape(b_, s_, GLA_HEADS, GLA_DK)
    vh = v.astype(jnp.float32).reshape(b_, s_, GLA_HEADS, GLA_DV)
    log_a = jax.nn.log_sigmoid((a_lr @ w_a_up + b_a).astype(jnp.float32)) / GLA_TAU
    log_a = log_a.reshape(b_, s_, GLA_HEADS, GLA_DK)
    o = gla_chunk_scan(qh, kh, vh, log_a)
    o = o * lax.rsqrt(jnp.mean(o * o, axis=-1, keepdims=True) + EPS)
    o = o * onorm_g.astype(jnp.float32).reshape(GLA_HEADS, GLA_DV)
    o = o.astype(dt).reshape(b_, s_, GLA_V) * jax.nn.silu(og)
    return o @ w_out


def swiglu(h, w_gate, w_up, w_down):
    return (jax.nn.silu(h @ w_gate) * (h @ w_up)) @ w_down


def moe_swiglu(h, w_router, b_router, w_gate, w_up, w_down):
    b_, s_, d = h.shape
    t = h.reshape(-1, d)
    logits = (t @ w_router).astype(jnp.float32) + b_router.astype(jnp.float32)
    top_logits, top_idx = lax.top_k(logits, TOP_K)
    top_w = jax.nn.softmax(top_logits, axis=-1)
    combine = jnp.sum(jax.nn.one_hot(top_idx, N_EXPERTS, dtype=jnp.float32) * top_w[..., None], axis=1)
    combine = combine.astype(h.dtype)
    y = jnp.zeros_like(t)
    for e in range(N_EXPERTS):
        y = y + combine[:, e:e + 1] * swiglu(t, w_gate[e], w_up[e], w_down[e])
    return y.reshape(b_, s_, d)


def setup_inputs(seed: int = 0) -> dict:
    key = jax.random.key(seed)
    ks = jax.random.split(key, 28)
    f32 = jnp.float32

    def nrm(i, shape, fan_in):
        return jax.random.normal(ks[i], shape, f32) * (fan_in ** -0.5)

    def gain(i, shape):
        return 1.0 + 0.02 * jax.random.normal(ks[i], shape, f32)

    def small(i, shape, s=0.02):
        return s * jax.random.normal(ks[i], shape, f32)

    return {
        "x": jax.random.normal(ks[0], (BATCH, SEQ, D_MODEL), f32),
        "p": jax.random.normal(ks[1], (DEPTH, BATCH, SEQ, PLE_DIM), f32),
        "norm_mix": gain(2, (DEPTH, D_MODEL)),
        "w_in": nrm(3, (DEPTH, D_MODEL, IN_COLS), D_MODEL),
        "conv_dw": nrm(4, (DEPTH, CONV_WIDTH, CONV_DIM), CONV_WIDTH),
        "conv_dw_b": small(5, (DEPTH, CONV_DIM)),
        "conv_ln_g": gain(6, (DEPTH, CONV_DIM)),
        "conv_ln_b": small(7, (DEPTH, CONV_DIM)),
        "conv_w_out": nrm(8, (DEPTH, CONV_DIM, D_MODEL), CONV_DIM),
        "gla_a_up": nrm(9, (DEPTH, GLA_GATE_RANK, GLA_K), GLA_GATE_RANK),
        "gla_a_b": small(10, (DEPTH, GLA_K), 0.5),
        "gla_onorm": gain(11, (DEPTH, GLA_V)),
        "gla_w_out": nrm(12, (DEPTH, GLA_V, D_MODEL), GLA_V),
        "w_o": nrm(13, (DEPTH, D_MODEL, D_MODEL), D_MODEL),
        "norm_ffn": gain(14, (DEPTH, D_MODEL)),
        "ffn_w_gate": nrm(15, (N_DENSE, D_MODEL, D_FF_DENSE), D_MODEL),
        "ffn_w_up": nrm(16, (N_DENSE, D_MODEL, D_FF_DENSE), D_MODEL),
        "ffn_w_down": nrm(17, (N_DENSE, D_FF_DENSE, D_MODEL), D_FF_DENSE),
        "router_w": nrm(18, (N_MOE, D_MODEL, N_EXPERTS), D_MODEL),
        "router_b": small(19, (N_MOE, N_EXPERTS), 0.01),
        "moe_w_gate": nrm(20, (N_MOE, N_EXPERTS, D_MODEL, D_FF_EXPERT), D_MODEL),
        "moe_w_up": nrm(21, (N_MOE, N_EXPERTS, D_MODEL, D_FF_EXPERT), D_MODEL),
        "moe_w_down": nrm(22, (N_MOE, N_EXPERTS, D_FF_EXPERT, D_MODEL), D_FF_EXPERT),
        "ple_norm": gain(23, (DEPTH, D_MODEL)),
        "ple_gate_down": nrm(24, (DEPTH, D_MODEL, PLE_DIM), D_MODEL),
        "ple_gate_up": nrm(25, (DEPTH, PLE_DIM, D_MODEL), PLE_DIM),
        "ple_proj": nrm(26, (DEPTH, PLE_DIM, D_MODEL), PLE_DIM),
        "final_norm": gain(27, (D_MODEL,)),
    }


def reference(x, p, norm_mix, w_in, conv_dw, conv_dw_b, conv_ln_g, conv_ln_b, conv_w_out,
              gla_a_up, gla_a_b, gla_onorm, gla_w_out, w_o, norm_ffn,
              ffn_w_gate, ffn_w_up, ffn_w_down, router_w, router_b,
              moe_w_gate, moe_w_up, moe_w_down,
              ple_norm, ple_gate_down, ple_gate_up, ple_proj, final_norm):
    for i in range(DEPTH):
        h = rmsnorm(x, norm_mix[i])
        z = h @ w_in[i]
        u_glu, q, k, v, og, a_lr, gate_logits = split_in(z)
        y_a = conformer_conv(u_glu, conv_dw[i], conv_dw_b[i], conv_ln_g[i], conv_ln_b[i], conv_w_out[i])
        y_b = gla_branch(q, k, v, og, a_lr, gla_a_up[i], gla_a_b[i], gla_onorm[i], gla_w_out[i])
        g_a, g_b = jnp.split(jax.nn.sigmoid(gate_logits), 2, axis=-1)
        x = x + (g_a * y_a + g_b * y_b) @ w_o[i]
        h = rmsnorm(x, norm_ffn[i])
        j = i // 2
        if i % 2 == 0:
            x = x + swiglu(h, ffn_w_gate[j], ffn_w_up[j], ffn_w_down[j])
        else:
            x = x + moe_swiglu(h, router_w[j], router_b[j], moe_w_gate[j], moe_w_up[j], moe_w_down[j])
        gate = jax.nn.sigmoid(rmsnorm(x, ple_norm[i]) @ ple_gate_down[i] @ ple_gate_up[i])
        x = x + (p[i] @ ple_proj[i]) * gate
    return rmsnorm(x, final_norm)
```

```python
import functools

import numpy as np
import jax
import jax.numpy as jnp
from jax import lax
from jax.experimental import pallas as pl
from jax.experimental.pallas import tpu as pltpu

F32 = jnp.float32
BF16 = jnp.bfloat16

EPS = 1e-6
CONV_WIDTH = 31
CONV_HALO = 32
GLA_HEADS = 8
GLA_DK = 128
GLA_DV = 256
GLA_RANK = 16
GLA_TAU = 16.0
GLA_CHUNK = 64
GLA_SUB = 16
N_EXPERTS = 8

V7X_LANES = 128
V7X_SCOPED_VMEM_CAP = 60000 * 1024


def _vmem_limit(block_bytes, temp_bytes=0):
    need = 2 * sum(block_bytes) + temp_bytes
    return int(min(V7X_SCOPED_VMEM_CAP, need + need // 4 + (4 << 20)))


def _nbytes(shape, dtype):
    return int(np.prod(shape)) * jnp.dtype(dtype).itemsize


def _params(n_grid, block_bytes, temp_bytes=0):
    return pltpu.CompilerParams(
        dimension_semantics=("arbitrary",) * n_grid,
        vmem_limit_bytes=_vmem_limit(block_bytes, temp_bytes))


def _rms_kernel(x_ref, g_ref, o_ref):
    x = x_ref[...]
    ms = jnp.mean(x * x, axis=-1, keepdims=True)
    o_ref[...] = (x * lax.rsqrt(ms + EPS) * g_ref[...]).astype(o_ref.dtype)


def rmsnorm(x, g, out_dtype, tm=256):
    t, d = x.shape
    blocks = [_nbytes((tm, d), F32), _nbytes((tm, d), out_dtype)]
    return pl.pallas_call(
        _rms_kernel,
        grid=(t // tm,),
        in_specs=[pl.BlockSpec((tm, d), lambda i: (i, 0)),
                  pl.BlockSpec((1, d), lambda i: (0, 0))],
        out_specs=pl.BlockSpec((tm, d), lambda i: (i, 0)),
        out_shape=jax.ShapeDtypeStruct((t, d), out_dtype),
        compiler_params=_params(1, blocks, 3 * _nbytes((tm, d), F32)),
        name="rmsnorm",
    )(x, g.reshape(1, d))


def _rms_router_kernel(x_ref, g_ref, wr_ref, br_ref, h_ref, comb_ref):
    x = x_ref[...]
    ms = jnp.mean(x * x, axis=-1, keepdims=True)
    h = x * lax.rsqrt(ms + EPS) * g_ref[...]
    h_ref[...] = h.astype(h_ref.dtype)
    logits = jnp.dot(h, wr_ref[...], preferred_element_type=F32,
                     precision=lax.Precision.HIGHEST) + br_ref[...]
    lane = lax.broadcasted_iota(jnp.int32, logits.shape, 1)
    logits = jnp.where(lane < N_EXPERTS, logits, -jnp.inf)
    m1 = jnp.max(logits, axis=-1, keepdims=True)
    i1 = jnp.min(jnp.where(logits == m1, lane, V7X_LANES), axis=-1, keepdims=True)
    rest = jnp.where(lane == i1, -jnp.inf, logits)
    m2 = jnp.max(rest, axis=-1, keepdims=True)
    i2 = jnp.min(jnp.where(rest == m2, lane, V7X_LANES), axis=-1, keepdims=True)
    e2 = jnp.exp(m2 - m1)
    w1 = 1.0 / (1.0 + e2)
    w2 = e2 / (1.0 + e2)
    comb_ref[...] = jnp.where(lane == i1, w1, 0.0) + jnp.where(lane == i2, w2, 0.0)


def rmsnorm_router(x, g, router_w, router_b, tm=256):
    t, d = x.shape
    wr = jnp.zeros((d, V7X_LANES), F32).at[:, :N_EXPERTS].set(router_w)
    br = jnp.zeros((1, V7X_LANES), F32).at[0, :N_EXPERTS].set(router_b)
    blocks = [_nbytes((tm, d), F32), _nbytes((tm, d), BF16), _nbytes((d, V7X_LANES), F32)]
    return pl.pallas_call(
        _rms_router_kernel,
        grid=(t // tm,),
        in_specs=[pl.BlockSpec((tm, d), lambda i: (i, 0)),
                  pl.BlockSpec((1, d), lambda i: (0, 0)),
                  pl.BlockSpec((d, V7X_LANES), lambda i: (0, 0)),
                  pl.BlockSpec((1, V7X_LANES), lambda i: (0, 0))],
        out_specs=[pl.BlockSpec((tm, d), lambda i: (i, 0)),
                   pl.BlockSpec((tm, V7X_LANES), lambda i: (i, 0))],
        out_shape=[jax.ShapeDtypeStruct((t, d), BF16),
                   jax.ShapeDtypeStruct((t, V7X_LANES), F32)],
        compiler_params=_params(1, blocks, 3 * _nbytes((tm, d), F32)),
        name="rmsnorm_router",
    )(x, g.reshape(1, d), wr, br)


def _mm_kernel(*refs, pair_a, n_a, n_t, epilogue):
    a_refs = refs[:n_a]
    w_refs = refs[n_a:n_a + len(pair_a)]
    t_refs = refs[n_a + len(pair_a):n_a + len(pair_a) + n_t]
    o_ref = refs[-1]
    accs = [jnp.dot(a_refs[ai][...], w_ref[...], preferred_element_type=F32)
            for ai, w_ref in zip(pair_a, w_refs)]
    tiles = [t_ref[...] for t_ref in t_refs]
    o_ref[...] = epilogue(accs, tiles, pl.program_id(1)).astype(o_ref.dtype)


def fused_matmul(a_list, w_list, t_list, epilogue, *, n_out, tm, tn, out_dtype, name, pair_a=None):
    m = a_list[0].shape[0]
    pair_a = tuple(range(len(w_list))) if pair_a is None else tuple(pair_a)
    in_specs = [pl.BlockSpec((tm, a.shape[1]), lambda i, j: (i, 0)) for a in a_list]
    in_specs += [pl.BlockSpec(bs, im) for _, bs, im in w_list]
    in_specs += [pl.BlockSpec(bs, im) for _, bs, im in t_list]
    blocks = [_nbytes((tm, a.shape[1]), a.dtype) for a in a_list]
    blocks += [_nbytes([s for s in bs if s is not None], w.dtype) for w, bs, _ in w_list]
    blocks += [_nbytes(bs, t.dtype) for t, bs, _ in t_list]
    blocks += [_nbytes((tm, tn), out_dtype)]
    temps = (len(w_list) + 1) * _nbytes((tm, tn), F32)
    return pl.pallas_call(
        functools.partial(_mm_kernel, pair_a=pair_a, n_a=len(a_list), n_t=len(t_list), epilogue=epilogue),
        grid=(m // tm, n_out // tn),
        in_specs=in_specs,
        out_specs=pl.BlockSpec((tm, tn), lambda i, j: (i, j)),
        out_shape=jax.ShapeDtypeStruct((m, n_out), out_dtype),
        compiler_params=_params(2, blocks, temps),
        name=name,
    )(*a_list, *[w for w, _, _ in w_list], *[t for t, _, _ in t_list])


def _col_tile(k, tn, off=0):
    return (k, tn), lambda i, j: (0, j + off)


def _ep_plain(accs, tiles, j):
    return accs[0]


def _ep_glu(accs, tiles, j):
    return accs[0] * jax.nn.sigmoid(accs[1])


def _ep_swiglu(accs, tiles, j):
    return jax.nn.silu(accs[0]) * accs[1]


def _ep_inproj(accs, tiles, j, *, silu_from, sigmoid_from):
    z = accs[0]
    s = jax.nn.sigmoid(z)
    return jnp.where(j >= sigmoid_from, s, jnp.where(j >= silu_from, z * s, z))


def _ep_gated_sum(accs, tiles, j):
    return accs[0] * tiles[0].astype(F32) + accs[1] * tiles[1].astype(F32)


def _ep_residual(accs, tiles, j):
    return tiles[0] + accs[0]


def _ep_ple(accs, tiles, j):
    return tiles[0] + accs[0] * jax.nn.sigmoid(accs[1])


def _ep_moe_up(accs, tiles, j, *, tiles_per_expert):
    comb = tiles[0]
    lane = lax.broadcasted_iota(jnp.int32, comb.shape, 1)
    c = jnp.sum(jnp.where(lane == j // tiles_per_expert, comb, 0.0), axis=-1, keepdims=True)
    return jax.nn.silu(accs[0]) * accs[1] * c


def _conv_kernel(u_ref, halo_ref, w_ref, b_ref, g_ref, beta_ref, o_ref, xs_ref, y_ref, *, tm, seq, rows):
    c = u_ref.shape[1]
    ext = tm + CONV_HALO - 8
    first = (pl.program_id(0) * tm) % seq == 0
    xs_ref[0, 0:CONV_HALO, :] = jnp.where(first, 0.0, halo_ref[...])
    xs_ref[0, CONV_HALO:CONV_HALO + tm, :] = u_ref[...]
    for s in range(1, 8):
        for c0 in range(0, c, V7X_LANES):
            xs_ref[s, 0:ext, c0:c0 + V7X_LANES] = xs_ref[0, s:s + ext, c0:c0 + V7X_LANES]

    lead = CONV_HALO - (CONV_WIDTH - 1)

    def conv_rows(r, carry):
        r0 = pl.multiple_of(r * rows, rows)
        for c0 in range(0, c, V7X_LANES):
            acc = jnp.broadcast_to(b_ref[:, c0:c0 + V7X_LANES], (rows, V7X_LANES))
            for w in range(CONV_WIDTH):
                off = lead + w
                win = xs_ref[off % 8, pl.ds(r0 + (off // 8) * 8, rows), c0:c0 + V7X_LANES]
                acc = acc + win * w_ref[w:w + 1, c0:c0 + V7X_LANES]
            y_ref[pl.ds(r0, rows), c0:c0 + V7X_LANES] = acc
        return carry

    lax.fori_loop(0, tm // rows, conv_rows, 0)

    def norm_rows(r, carry):
        r0 = pl.multiple_of(r * 32, 32)
        y = y_ref[pl.ds(r0, 32), :]
        mu = jnp.mean(y, axis=-1, keepdims=True)
        d = y - mu
        var = jnp.mean(d * d, axis=-1, keepdims=True)
        z = d * lax.rsqrt(var + EPS) * g_ref[...] + beta_ref[...]
        o_ref[pl.ds(r0, 32), :] = jax.nn.silu(z).astype(o_ref.dtype)
        return carry

    lax.fori_loop(0, tm // 32, norm_rows, 0)


def conv_branch(u, w_dw, b_dw, ln_g, ln_b, *, seq, tm=256, rows=128):
    t, c = u.shape
    assert seq % tm == 0 and tm % rows == 0 and tm % CONV_HALO == 0
    hb = tm // CONV_HALO
    blocks = [_nbytes((tm, c), F32), _nbytes((CONV_HALO, c), F32), _nbytes((CONV_WIDTH, c), F32),
              _nbytes((tm, c), BF16)]
    scratch = 8 * _nbytes((tm + CONV_HALO, c), F32) + _nbytes((tm, c), F32)
    return pl.pallas_call(
        functools.partial(_conv_kernel, tm=tm, seq=seq, rows=rows),
        grid=(t // tm,),
        in_specs=[pl.BlockSpec((tm, c), lambda i: (i, 0)),
                  pl.BlockSpec((CONV_HALO, c), lambda i: (jnp.maximum(i * hb - 1, 0), 0)),
                  pl.BlockSpec((CONV_WIDTH, c), lambda i: (0, 0)),
                  pl.BlockSpec((1, c), lambda i: (0, 0)),
                  pl.BlockSpec((1, c), lambda i: (0, 0)),
                  pl.BlockSpec((1, c), lambda i: (0, 0))],
        out_specs=pl.BlockSpec((tm, c), lambda i: (i, 0)),
        out_shape=jax.ShapeDtypeStruct((t, c), BF16),
        scratch_shapes=[pltpu.VMEM((8, tm + CONV_HALO, c), F32), pltpu.VMEM((tm, c), F32)],
        compiler_params=_params(1, blocks, scratch),
        name="conv_branch",
    )(u, u, w_dw, b_dw.reshape(1, c), ln_g.reshape(1, c), ln_b.reshape(1, c))


def _gla_constants():
    cn, sb = GLA_CHUNK, GLA_SUB
    r = np.arange(cn)[:, None]
    c = np.arange(cn)[None, :]
    blk = r // sb
    tri = c <= r
    parts = [tri, tri & (c >= blk * sb), (c > r) & (c < (blk + 1) * sb), c > r]
    parts += [(c >= (j + 1) * sb) & (c < blk * sb) for j in range(cn // sb - 1)]
    parts += [np.ones((16, cn), bool)]
    parts += [(c > r - l) & (c <= r) for l in range(sb)]
    sums = np.concatenate(parts).astype(np.float32)
    shifts = np.concatenate([c == r - l for l in range(sb)]).astype(np.float32)
    lag_of = np.where((c <= r) & (c // sb == blk), r - c, -1).astype(np.int32)
    return sums, shifts, lag_of


def _split3(x):
    hi = x.astype(BF16)
    r1 = x - hi.astype(F32)
    mid = r1.astype(BF16)
    lo = (r1 - mid.astype(F32)).astype(BF16)
    return hi, mid, lo


def _gla_kernel(q_ref, k_ref, v_ref, og_ref, alr_ref, wup_ref, bup_ref, gn_ref,
                sums_ref, shifts_ref, lag_ref, o_ref, st_ref, la_ref, *, n_chunks):
    cn, sb = GLA_CHUNK, GLA_SUB
    n_sub = cn // sb

    @pl.when(pl.program_id(2) == 0)
    def _():
        st_ref[...] = jnp.zeros_like(st_ref)

    x = jnp.dot(alr_ref[...].astype(BF16), wup_ref[...], preferred_element_type=F32) + bup_ref[...]
    la_ref[...] = (jnp.minimum(x, 0.0) - jnp.log(1.0 + jnp.exp(-jnp.abs(x)))) * (1.0 / GLA_TAU)

    row_blk = lax.broadcasted_iota(jnp.int32, (cn, GLA_DK), 0) // sb
    ones_dk = jnp.ones((GLA_DK, cn), BF16)
    lag_of = lag_ref[...]

    def chunk(ci, carry):
        r0 = pl.multiple_of(ci * cn, cn)
        la = la_ref[pl.ds(r0, cn), :]
        q = q_ref[pl.ds(r0, cn), :].astype(F32) * (GLA_DK ** -0.5)
        k_bf = k_ref[pl.ds(r0, cn), :]
        k = k_bf.astype(F32)
        v = v_ref[pl.ds(r0, cn), :]

        hi, mid, lo = _split3(la)
        sums = sums_ref[...]
        ex = jnp.exp(jnp.dot(sums, hi, preferred_element_type=F32)
                     + jnp.dot(sums, mid, preferred_element_type=F32)
                     + jnp.dot(sums, lo, preferred_element_type=F32))
        q_state = q * ex[0:cn]
        q_sub = q * ex[cn:2 * cn]
        k_sub = k * ex[2 * cn:3 * cn]
        k_dec = k * ex[3 * cn:4 * cn]
        off_lag = (4 + n_sub - 1) * cn + 16
        e_last = ex[off_lag - 16:off_lag - 15]

        lhs = jnp.concatenate(
            [jnp.where(row_blk > j, q_sub * ex[(4 + j) * cn:(5 + j) * cn], 0.0) for j in range(n_sub - 1)], axis=1)
        rhs = jnp.concatenate(
            [jnp.where(row_blk == j, k_sub, 0.0) for j in range(n_sub - 1)], axis=1)
        scores = lax.dot_general(lhs.astype(BF16), rhs.astype(BF16), (((1,), (1,)), ((), ())),
                                 preferred_element_type=F32)

        k_shift = jnp.dot(shifts_ref[...], k_bf, preferred_element_type=F32)
        terms = jnp.concatenate(
            [q * k_shift[l * cn:(l + 1) * cn] * ex[off_lag + l * cn:off_lag + (l + 1) * cn] for l in range(sb)],
            axis=0)
        lag_sums = jnp.dot(terms.astype(BF16), ones_dk, preferred_element_type=F32)
        for l in range(sb):
            scores = jnp.where(lag_of == l, lag_sums[l * cn:(l + 1) * cn], scores)

        st = st_ref[...]
        o = (jnp.dot(scores.astype(BF16), v, preferred_element_type=F32)
             + lax.dot_general(q_state.astype(BF16), st.astype(BF16), (((1,), (1,)), ((), ())),
                               preferred_element_type=F32))
        st_ref[...] = st * e_last + lax.dot_general(v, k_dec.astype(BF16), (((0,), (0,)), ((), ())),
                                                    preferred_element_type=F32)

        o = o * lax.rsqrt(jnp.mean(o * o, axis=-1, keepdims=True) + EPS) * gn_ref[...]
        o_ref[pl.ds(r0, cn), :] = (o * og_ref[pl.ds(r0, cn), :].astype(F32)).astype(o_ref.dtype)
        return carry

    lax.fori_loop(0, n_chunks, chunk, 0)


def gla_branch(z, a_lr, w_a_up, b_a, onorm_g, *, batch, seq, q_col, k_col, v_col, og_col, rows=1024):
    t = z.shape[0]
    assert seq % rows == 0 and rows % GLA_CHUNK == 0
    nb = seq // rows
    sums, shifts, lag_of = _gla_constants()
    wup = jnp.zeros((V7X_LANES, GLA_HEADS * GLA_DK), F32).at[:GLA_RANK].set(w_a_up).astype(BF16)
    qb, kb, vb, ob = q_col // GLA_DK, k_col // GLA_DK, v_col // GLA_DV, og_col // GLA_DV

    def row(b, h, c):
        return b * nb + c

    blocks = [2 * _nbytes((rows, GLA_DK), BF16), 3 * _nbytes((rows, GLA_DV), BF16),
              _nbytes((rows, V7X_LANES), F32), _nbytes(sums.shape, BF16) * 2, _nbytes(shifts.shape, BF16) * 2]
    scratch = _nbytes((GLA_DV, GLA_DK), F32) + _nbytes((rows, GLA_DK), F32)
    return pl.pallas_call(
        functools.partial(_gla_kernel, n_chunks=rows // GLA_CHUNK),
        grid=(batch, GLA_HEADS, nb),
        in_specs=[pl.BlockSpec((rows, GLA_DK), lambda b, h, c: (row(b, h, c), qb + h)),
                  pl.BlockSpec((rows, GLA_DK), lambda b, h, c: (row(b, h, c), kb + h)),
                  pl.BlockSpec((rows, GLA_DV), lambda b, h, c: (row(b, h, c), vb + h)),
                  pl.BlockSpec((rows, GLA_DV), lambda b, h, c: (row(b, h, c), ob + h)),
                  pl.BlockSpec((rows, V7X_LANES), lambda b, h, c: (row(b, h, c), 0)),
                  pl.BlockSpec((V7X_LANES, GLA_DK), lambda b, h, c: (0, h)),
                  pl.BlockSpec((1, GLA_DK), lambda b, h, c: (0, h)),
                  pl.BlockSpec((1, GLA_DV), lambda b, h, c: (0, h)),
                  pl.BlockSpec(sums.shape, lambda b, h, c: (0, 0)),
                  pl.BlockSpec(shifts.shape, lambda b, h, c: (0, 0)),
                  pl.BlockSpec(lag_of.shape, lambda b, h, c: (0, 0))],
        out_specs=pl.BlockSpec((rows, GLA_DV), lambda b, h, c: (row(b, h, c), h)),
        out_shape=jax.ShapeDtypeStruct((t, GLA_HEADS * GLA_DV), BF16),
        scratch_shapes=[pltpu.VMEM((GLA_DV, GLA_DK), F32), pltpu.VMEM((rows, GLA_DK), F32)],
        compiler_params=_params(3, blocks, scratch + (8 << 20)),
        name="gla_branch",
    )(z, z, z, z, a_lr, wup, b_a.reshape(1, -1), onorm_g.reshape(1, -1),
      jnp.asarray(sums, BF16), jnp.asarray(shifts, BF16), jnp.asarray(lag_of))


def _layer(x, p_bf, lw, *, batch, seq, moe):
    t, d = x.shape
    cdim = lw["conv_dw"].shape[1]
    kdim, vdim = GLA_HEADS * GLA_DK, GLA_HEADS * GLA_DV
    tm, tn = 1024, 512

    h = rmsnorm(x, lw["norm_mix"], BF16)
    w_in = lw["w_in"]
    o_q = 2 * cdim
    o_alr = o_q + 2 * kdim + 2 * vdim
    o_gate = o_alr + GLA_RANK
    w_glu = w_in[:, :o_q].astype(BF16)
    w_z = jnp.concatenate([w_in[:, o_q:o_alr], w_in[:, o_gate:]], axis=1).astype(BF16)
    w_alr = jnp.zeros((d, V7X_LANES), BF16).at[:, :GLA_RANK].set(w_in[:, o_alr:o_gate].astype(BF16))

    u = fused_matmul([h], [(w_glu, *_col_tile(d, tn)), (w_glu, *_col_tile(d, tn, cdim // tn))], [],
                     _ep_glu, n_out=cdim, tm=tm, tn=tn, out_dtype=F32, name="in_glu", pair_a=(0, 0))
    n_z = w_z.shape[1]
    c_og = 2 * kdim + vdim
    c_ga = c_og + vdim
    z = fused_matmul([h], [(w_z, *_col_tile(d, tn))], [],
                     functools.partial(_ep_inproj, silu_from=c_og // tn, sigmoid_from=c_ga // tn),
                     n_out=n_z, tm=tm, tn=tn, out_dtype=BF16, name="in_z")
    a_lr = fused_matmul([h], [(w_alr, *_col_tile(d, V7X_LANES))], [], _ep_plain,
                        n_out=V7X_LANES, tm=tm, tn=V7X_LANES, out_dtype=F32, name="in_alr")

    y_conv = conv_branch(u, lw["conv_dw"], lw["conv_dw_b"], lw["conv_ln_g"], lw["conv_ln_b"], seq=seq)
    y_gla = gla_branch(z, a_lr, lw["gla_a_up"], lw["gla_a_b"], lw["gla_onorm"], batch=batch, seq=seq,
                       q_col=0, k_col=kdim, v_col=2 * kdim, og_col=c_og)

    def z_tile(off):
        return (tm, tn), lambda i, j: (i, j + off // tn)

    mix = fused_matmul([y_conv, y_gla],
                       [(lw["conv_w_out"].astype(BF16), *_col_tile(cdim, tn)),
                        (lw["gla_w_out"].astype(BF16), *_col_tile(vdim, tn))],
                       [(z, *z_tile(c_ga)), (z, *z_tile(c_ga + d))],
                       _ep_gated_sum, n_out=d, tm=tm, tn=tn, out_dtype=BF16, name="branch_out")

    def x_tile():
        return (tm, tn), lambda i, j: (i, j)

    x = fused_matmul([mix], [(lw["w_o"].astype(BF16), *_col_tile(d, tn))], [(x, *x_tile())],
                     _ep_residual, n_out=d, tm=tm, tn=tn, out_dtype=F32, name="w_o")

    if not moe:
        h = rmsnorm(x, lw["norm_ffn"], BF16)
        f = lw["ffn_w_gate"].shape[1]
        act = fused_matmul([h], [(lw["ffn_w_gate"].astype(BF16), *_col_tile(d, tn)),
                                 (lw["ffn_w_up"].astype(BF16), *_col_tile(d, tn))], [],
                           _ep_swiglu, n_out=f, tm=tm, tn=tn, out_dtype=BF16, name="ffn_up", pair_a=(0, 0))
        x = fused_matmul([act], [(lw["ffn_w_down"].astype(BF16), *_col_tile(f, tn))], [(x, *x_tile())],
                         _ep_residual, n_out=d, tm=tm, tn=tn, out_dtype=F32, name="ffn_down")
    else:
        h, comb = rmsnorm_router(x, lw["norm_ffn"], lw["router_w"], lw["router_b"])
        ne, _, fe = lw["moe_w_gate"].shape
        tpe = fe // tn

        def expert_tile():
            return (None, d, tn), lambda i, j: (j // tpe, 0, j % tpe)

        act = fused_matmul([h], [(lw["moe_w_gate"].astype(BF16), *expert_tile()),
                                 (lw["moe_w_up"].astype(BF16), *expert_tile())],
                           [(comb, (tm, V7X_LANES), lambda i, j: (i, 0))],
                           functools.partial(_ep_moe_up, tiles_per_expert=tpe),
                           n_out=ne * fe, tm=tm, tn=tn, out_dtype=BF16, name="moe_up", pair_a=(0, 0))
        tmd = 512
        x = fused_matmul([act], [(lw["moe_w_down"].astype(BF16).reshape(ne * fe, d), *_col_tile(ne * fe, tn))],
                         [(x, (tmd, tn), lambda i, j: (i, j))],
                         _ep_residual, n_out=d, tm=tmd, tn=tn, out_dtype=F32, name="moe_down")

    h = rmsnorm(x, lw["ple_norm"], BF16)
    pd = lw["ple_gate_down"].shape[1]
    g1 = fused_matmul([h], [(lw["ple_gate_down"].astype(BF16), *_col_tile(d, pd))], [], _ep_plain,
                      n_out=pd, tm=tm, tn=pd, out_dtype=BF16, name="ple_down")
    x = fused_matmul([p_bf, g1], [(lw["ple_proj"].astype(BF16), *_col_tile(pd, tn)),
                                  (lw["ple_gate_up"].astype(BF16), *_col_tile(pd, tn))],
                     [(x, *x_tile())], _ep_ple, n_out=d, tm=tm, tn=tn, out_dtype=F32, name="ple_up")
    return x


def kernel(x, p, norm_mix, w_in, conv_dw, conv_dw_b, conv_ln_g, conv_ln_b, conv_w_out, gla_a_up, gla_a_b, gla_onorm, gla_w_out, w_o, norm_ffn, ffn_w_gate, ffn_w_up, ffn_w_down, router_w, router_b, moe_w_gate, moe_w_up, moe_w_down, ple_norm, ple_gate_down, ple_gate_up, ple_proj, final_norm):
    batch, seq, d = x.shape
    depth = w_in.shape[0]
    xt = x.reshape(batch * seq, d)
    for i in range(depth):
        j = i // 2
        lw = dict(norm_mix=norm_mix[i], w_in=w_in[i], conv_dw=conv_dw[i], conv_dw_b=conv_dw_b[i],
                  conv_ln_g=conv_ln_g[i], conv_ln_b=conv_ln_b[i], conv_w_out=conv_w_out[i],
                  gla_a_up=gla_a_up[i], gla_a_b=gla_a_b[i], gla_onorm=gla_onorm[i], gla_w_out=gla_w_out[i],
                  w_o=w_o[i], norm_ffn=norm_ffn[i], ple_norm=ple_norm[i], ple_gate_down=ple_gate_down[i],
                  ple_gate_up=ple_gate_up[i], ple_proj=ple_proj[i])
        if i % 2 == 0:
            lw.update(ffn_w_gate=ffn_w_gate[j], ffn_w_up=ffn_w_up[j], ffn_w_down=ffn_w_down[j])
        else:
            lw.update(router_w=router_w[j], router_b=router_b[j], moe_w_gate=moe_w_gate[j],
                      moe_w_up=moe_w_up[j], moe_w_down=moe_w_down[j])
        p_bf = p[i].reshape(batch * seq, -1).astype(BF16)
        xt = _layer(xt, p_bf, lw, batch=batch, seq=seq, moe=(i % 2 == 1))
    return rmsnorm(xt, final_norm, F32).reshape(batch, seq, d)
```

```python
import functools

import numpy as np
import jax
import jax.numpy as jnp
from jax import lax
from jax.experimental import pallas as pl
from jax.experimental.pallas import tpu as pltpu

F32 = jnp.float32
BF16 = jnp.bfloat16

EPS = 1e-6
CONV_WIDTH = 31
CONV_HALO = 32
CONV_NORM_ROWS = 64
CONV_ROWS = 32
CONV_COLS = 512
GLA_HEADS = 8
GLA_DK = 128
GLA_DV = 256
GLA_RANK = 16
GLA_TAU = 16.0
GLA_CHUNK = 64
GLA_LEVELS = 6
N_EXPERTS = 8

V7X_LANES = 128
V7X_SCOPED_VMEM_CAP = 60000 * 1024


def _vmem_limit(block_bytes, temp_bytes=0):
    need = 2 * sum(block_bytes) + temp_bytes
    return int(min(V7X_SCOPED_VMEM_CAP, need + need // 4 + (4 << 20)))


def _nbytes(shape, dtype):
    return int(np.prod(shape)) * jnp.dtype(dtype).itemsize


def _params(n_grid, block_bytes, temp_bytes=0):
    return pltpu.CompilerParams(
        dimension_semantics=("arbitrary",) * n_grid,
        vmem_limit_bytes=_vmem_limit(block_bytes, temp_bytes))


def _rms_kernel(x_ref, g_ref, o_ref):
    x = x_ref[...]
    ms = jnp.mean(x * x, axis=-1, keepdims=True)
    o_ref[...] = (x * lax.rsqrt(ms + EPS) * g_ref[...]).astype(o_ref.dtype)


def rmsnorm(x, g, out_dtype, tm=256):
    t, d = x.shape
    blocks = [_nbytes((tm, d), F32), _nbytes((tm, d), out_dtype)]
    return pl.pallas_call(
        _rms_kernel,
        grid=(t // tm,),
        in_specs=[pl.BlockSpec((tm, d), lambda i: (i, 0)),
                  pl.BlockSpec((1, d), lambda i: (0, 0))],
        out_specs=pl.BlockSpec((tm, d), lambda i: (i, 0)),
        out_shape=jax.ShapeDtypeStruct((t, d), out_dtype),
        compiler_params=_params(1, blocks, 3 * _nbytes((tm, d), F32)),
        name="rmsnorm",
    )(x, g.reshape(1, d))


def _rms_router_kernel(x_ref, g_ref, wr_ref, br_ref, h_ref, comb_ref):
    x = x_ref[...]
    ms = jnp.mean(x * x, axis=-1, keepdims=True)
    h = x * lax.rsqrt(ms + EPS) * g_ref[...]
    h_ref[...] = h.astype(h_ref.dtype)
    logits = jnp.dot(h, wr_ref[...], preferred_element_type=F32,
                     precision=lax.Precision.HIGHEST) + br_ref[...]
    lane = lax.broadcasted_iota(jnp.int32, logits.shape, 1)
    logits = jnp.where(lane < N_EXPERTS, logits, -jnp.inf)
    m1 = jnp.max(logits, axis=-1, keepdims=True)
    i1 = jnp.min(jnp.where(logits == m1, lane, V7X_LANES), axis=-1, keepdims=True)
    rest = jnp.where(lane == i1, -jnp.inf, logits)
    m2 = jnp.max(rest, axis=-1, keepdims=True)
    i2 = jnp.min(jnp.where(rest == m2, lane, V7X_LANES), axis=-1, keepdims=True)
    e2 = jnp.exp(m2 - m1)
    w1 = 1.0 / (1.0 + e2)
    w2 = e2 / (1.0 + e2)
    comb_ref[...] = jnp.where(lane == i1, w1, 0.0) + jnp.where(lane == i2, w2, 0.0)


def rmsnorm_router(x, g, router_w, router_b, tm=256):
    t, d = x.shape
    wr = jnp.zeros((d, V7X_LANES), F32).at[:, :N_EXPERTS].set(router_w)
    br = jnp.zeros((1, V7X_LANES), F32).at[0, :N_EXPERTS].set(router_b)
    blocks = [_nbytes((tm, d), F32), _nbytes((tm, d), BF16), _nbytes((d, V7X_LANES), F32)]
    return pl.pallas_call(
        _rms_router_kernel,
        grid=(t // tm,),
        in_specs=[pl.BlockSpec((tm, d), lambda i: (i, 0)),
                  pl.BlockSpec((1, d), lambda i: (0, 0)),
                  pl.BlockSpec((d, V7X_LANES), lambda i: (0, 0)),
                  pl.BlockSpec((1, V7X_LANES), lambda i: (0, 0))],
        out_specs=[pl.BlockSpec((tm, d), lambda i: (i, 0)),
                   pl.BlockSpec((tm, V7X_LANES), lambda i: (i, 0))],
        out_shape=[jax.ShapeDtypeStruct((t, d), BF16),
                   jax.ShapeDtypeStruct((t, V7X_LANES), F32)],
        compiler_params=_params(1, blocks, 3 * _nbytes((tm, d), F32)),
        name="rmsnorm_router",
    )(x, g.reshape(1, d), wr, br)


def _mm_kernel(*refs, pair_a, n_a, n_t, epilogue):
    a_refs = refs[:n_a]
    w_refs = refs[n_a:n_a + len(pair_a)]
    t_refs = refs[n_a + len(pair_a):n_a + len(pair_a) + n_t]
    o_ref = refs[-1]
    accs = [jnp.dot(a_refs[ai][...], w_ref[...], preferred_element_type=F32)
            for ai, w_ref in zip(pair_a, w_refs)]
    tiles = [t_ref[...] for t_ref in t_refs]
    o_ref[...] = epilogue(accs, tiles, pl.program_id(1)).astype(o_ref.dtype)


def fused_matmul(a_list, w_list, t_list, epilogue, *, n_out, tm, tn, out_dtype, name, pair_a=None):
    m = a_list[0].shape[0]
    pair_a = tuple(range(len(w_list))) if pair_a is None else tuple(pair_a)
    in_specs = [pl.BlockSpec((tm, a.shape[1]), lambda i, j: (i, 0)) for a in a_list]
    in_specs += [pl.BlockSpec(bs, im) for _, bs, im in w_list]
    in_specs += [pl.BlockSpec(bs, im) for _, bs, im in t_list]
    blocks = [_nbytes((tm, a.shape[1]), a.dtype) for a in a_list]
    blocks += [_nbytes([s for s in bs if s is not None], w.dtype) for w, bs, _ in w_list]
    blocks += [_nbytes(bs, t.dtype) for t, bs, _ in t_list]
    blocks += [_nbytes((tm, tn), out_dtype)]
    temps = (len(w_list) + 1) * _nbytes((tm, tn), F32)
    return pl.pallas_call(
        functools.partial(_mm_kernel, pair_a=pair_a, n_a=len(a_list), n_t=len(t_list), epilogue=epilogue),
        grid=(m // tm, n_out // tn),
        in_specs=in_specs,
        out_specs=pl.BlockSpec((tm, tn), lambda i, j: (i, j)),
        out_shape=jax.ShapeDtypeStruct((m, n_out), out_dtype),
        compiler_params=_params(2, blocks, temps),
        name=name,
    )(*a_list, *[w for w, _, _ in w_list], *[t for t, _, _ in t_list])


def _col_tile(k, tn, off=0):
    return (k, tn), lambda i, j: (0, j + off)


def _ep_plain(accs, tiles, j):
    return accs[0]


def _ep_glu(accs, tiles, j):
    return accs[0] * jax.nn.sigmoid(accs[1])


def _ep_swiglu(accs, tiles, j):
    return jax.nn.silu(accs[0]) * accs[1]


def _ep_inproj(accs, tiles, j, *, silu_from, sigmoid_from):
    z = accs[0]
    s = jax.nn.sigmoid(z)
    return jnp.where(j >= sigmoid_from, s, jnp.where(j >= silu_from, z * s, z))


def _ep_gated_sum(accs, tiles, j):
    return accs[0] * tiles[0].astype(F32) + accs[1] * tiles[1].astype(F32)


def _ep_residual(accs, tiles, j):
    return tiles[0] + accs[0]


def _ep_ple(accs, tiles, j):
    return tiles[0] + accs[0] * jax.nn.sigmoid(accs[1])


def _ep_moe_up(accs, tiles, j, *, tiles_per_expert):
    comb = tiles[0]
    lane = lax.broadcasted_iota(jnp.int32, comb.shape, 1)
    c = jnp.sum(jnp.where(lane == j // tiles_per_expert, comb, 0.0), axis=-1, keepdims=True)
    return jax.nn.silu(accs[0]) * accs[1] * c


def _conv_kernel(u_ref, halo_ref, w_ref, b_ref, g_ref, beta_ref, o_ref, xs_ref, y_ref, *, tm, seq):
    c = u_ref.shape[1]
    ext = tm + CONV_HALO - 8
    first = (pl.program_id(0) * tm) % seq == 0
    xs_ref[0, 0:CONV_HALO, :] = jnp.where(first, 0.0, halo_ref[...])
    xs_ref[0, CONV_HALO:CONV_HALO + tm, :] = u_ref[...]
    for s in range(1, 8):
        for c0 in range(0, c, V7X_LANES):
            xs_ref[s, 0:ext, c0:c0 + V7X_LANES] = xs_ref[0, s:s + ext, c0:c0 + V7X_LANES]

    lead = CONV_HALO - (CONV_WIDTH - 1)

    def conv_rows(r, carry):
        r0 = pl.multiple_of(r * CONV_ROWS, CONV_ROWS)
        for c0 in range(0, c, CONV_COLS):
            cs = slice(c0, c0 + CONV_COLS)
            accs = [b_ref[:, cs]] * (CONV_ROWS // 8)
            for w in range(CONV_WIDTH):
                off = lead + w
                base = r0 + (off // 8) * 8
                wv = w_ref[8 * w:8 * w + 8, cs]
                accs = [a + xs_ref[off % 8, pl.ds(base + 8 * g, 8), cs] * wv for g, a in enumerate(accs)]
            for g, a in enumerate(accs):
                y_ref[pl.ds(r0 + 8 * g, 8), cs] = a
        return carry

    lax.fori_loop(0, tm // CONV_ROWS, conv_rows, 0)

    def norm_rows(r, carry):
        r0 = pl.multiple_of(r * CONV_NORM_ROWS, CONV_NORM_ROWS)
        y = y_ref[pl.ds(r0, CONV_NORM_ROWS), :]
        mu = jnp.mean(y, axis=-1, keepdims=True)
        d = y - mu
        var = jnp.mean(d * d, axis=-1, keepdims=True)
        z = d * lax.rsqrt(var + EPS) * g_ref[...] + beta_ref[...]
        o_ref[pl.ds(r0, CONV_NORM_ROWS), :] = jax.nn.silu(z).astype(o_ref.dtype)
        return carry

    lax.fori_loop(0, tm // CONV_NORM_ROWS, norm_rows, 0)


def conv_branch(u, w_dw, b_dw, ln_g, ln_b, *, seq, tm=256):
    t, c = u.shape
    assert seq % tm == 0 and tm % CONV_HALO == 0 and tm % CONV_NORM_ROWS == 0
    hb = tm // CONV_HALO
    w_rep = jnp.repeat(w_dw, 8, axis=0)
    b_rep = jnp.broadcast_to(b_dw.reshape(1, c), (8, c))
    blocks = [_nbytes((tm, c), F32), _nbytes((CONV_HALO, c), F32), _nbytes(w_rep.shape, F32),
              _nbytes((tm, c), BF16)]
    scratch = 8 * _nbytes((tm + CONV_HALO, c), F32) + _nbytes((tm, c), F32)
    return pl.pallas_call(
        functools.partial(_conv_kernel, tm=tm, seq=seq),
        grid=(t // tm,),
        in_specs=[pl.BlockSpec((tm, c), lambda i: (i, 0)),
                  pl.BlockSpec((CONV_HALO, c), lambda i: (jnp.maximum(i * hb - 1, 0), 0)),
                  pl.BlockSpec(w_rep.shape, lambda i: (0, 0)),
                  pl.BlockSpec((8, c), lambda i: (0, 0)),
                  pl.BlockSpec((1, c), lambda i: (0, 0)),
                  pl.BlockSpec((1, c), lambda i: (0, 0))],
        out_specs=pl.BlockSpec((tm, c), lambda i: (i, 0)),
        out_shape=jax.ShapeDtypeStruct((t, c), BF16),
        scratch_shapes=[pltpu.VMEM((8, tm + CONV_HALO, c), F32), pltpu.VMEM((tm, c), F32)],
        compiler_params=_params(1, blocks, scratch),
        name="conv_branch",
    )(u, u, w_rep, b_rep, ln_g.reshape(1, c), ln_b.reshape(1, c))


def _gla_constants():
    cn = GLA_CHUNK
    r = np.arange(cn)[:, None]
    c = np.arange(cn)[None, :]
    parts = [c <= r, c > r]
    for lv in range(GLA_LEVELS):
        half = 1 << lv
        mid = (r // (2 * half)) * (2 * half) + half - 1
        right = (r % (2 * half)) >= half
        parts.append(np.where(right, (c > mid) & (c <= r), (c > r) & (c <= mid)))
    parts.append(np.ones((16, cn), bool))
    sums = np.concatenate(parts).astype(np.float32)
    high_bit = np.floor(np.log2(np.maximum(r ^ c, 1))).astype(np.int32)
    level_of = np.where(c < r, high_bit, np.where(c == r, GLA_LEVELS, -1)).astype(np.int32)
    return sums, level_of


def _gla_kernel(q_ref, k_ref, v_ref, og_ref, alr_ref, wup_ref, bup_ref, gn_ref,
                sums_ref, level_ref, o_ref, st_ref, la_ref, *, n_chunks, heads):
    cn = GLA_CHUNK

    @pl.when(pl.program_id(2) == 0)
    def _():
        st_ref[...] = jnp.zeros_like(st_ref)

    x = jnp.dot(alr_ref[...].astype(BF16), wup_ref[...], preferred_element_type=F32) + bup_ref[...]
    la_ref[...] = (jnp.minimum(x, 0.0) - jnp.log(1.0 + jnp.exp(-jnp.abs(x)))) * (1.0 / GLA_TAU)

    ones_dk = jnp.ones((GLA_DK, cn), BF16)
    level_of = level_ref[...]
    nt = (((1,), (1,)), ((), ()))
    tn = (((0,), (0,)), ((), ()))

    def chunk(ci, carry):
        r0 = pl.multiple_of(ci * cn, cn)
        la = la_ref[pl.ds(r0, cn), :]
        hi = la.astype(BF16)
        lo = (la - hi.astype(F32)).astype(BF16)
        sums = sums_ref[...]
        ex_all = jnp.exp(jnp.dot(sums, hi, preferred_element_type=F32)
                         + jnp.dot(sums, lo, preferred_element_type=F32))
        for h in range(heads):
            kc = slice(h * GLA_DK, (h + 1) * GLA_DK)
            vc = slice(h * GLA_DV, (h + 1) * GLA_DV)
            ex = ex_all[:, kc]
            q = q_ref[pl.ds(r0, cn), kc].astype(F32) * (GLA_DK ** -0.5)
            k = k_ref[pl.ds(r0, cn), kc].astype(F32)
            v = v_ref[pl.ds(r0, cn), vc]
            q_state = q * ex[0:cn]
            k_dec = k * ex[cn:2 * cn]
            e_last = ex[(2 + GLA_LEVELS) * cn:(2 + GLA_LEVELS) * cn + 1]

            diag = jnp.dot((q * k).astype(BF16), ones_dk, preferred_element_type=F32)
            scores = jnp.where(level_of == GLA_LEVELS, diag, 0.0)
            for lv in range(GLA_LEVELS):
                e = ex[(2 + lv) * cn:(3 + lv) * cn]
                s = lax.dot_general((q * e).astype(BF16), (k * e).astype(BF16), nt,
                                    preferred_element_type=F32)
                scores = jnp.where(level_of == lv, s, scores)

            st = st_ref[h]
            o = (jnp.dot(scores.astype(BF16), v, preferred_element_type=F32)
                 + lax.dot_general(q_state.astype(BF16), st.astype(BF16), nt, preferred_element_type=F32))
            st_ref[h] = st * e_last + lax.dot_general(v, k_dec.astype(BF16), tn, preferred_element_type=F32)

            o = o * lax.rsqrt(jnp.mean(o * o, axis=-1, keepdims=True) + EPS) * gn_ref[:, vc]
            o_ref[pl.ds(r0, cn), vc] = (o * og_ref[pl.ds(r0, cn), vc].astype(F32)).astype(o_ref.dtype)
        return carry

    lax.fori_loop(0, n_chunks, chunk, 0)


def gla_branch(z, a_lr, w_a_up, b_a, onorm_g, *, batch, seq, q_col, k_col, v_col, og_col, rows=1024, heads=8):
    t = z.shape[0]
    assert seq % rows == 0 and rows % GLA_CHUNK == 0 and GLA_HEADS % heads == 0
    nb = seq // rows
    wk, wv = heads * GLA_DK, heads * GLA_DV
    sums, level_of = _gla_constants()
    wup = jnp.zeros((V7X_LANES, GLA_HEADS * GLA_DK), F32).at[:GLA_RANK].set(w_a_up).astype(BF16)
    qb, kb, vb, ob = q_col // wk, k_col // wk, v_col // wv, og_col // wv

    def row(b, g, c):
        return b * nb + c

    blocks = [2 * _nbytes((rows, wk), BF16), 3 * _nbytes((rows, wv), BF16),
              _nbytes((rows, V7X_LANES), F32), _nbytes(sums.shape, BF16) * 2]
    scratch = heads * _nbytes((GLA_DV, GLA_DK), F32) + _nbytes((rows, wk), F32)
    return pl.pallas_call(
        functools.partial(_gla_kernel, n_chunks=rows // GLA_CHUNK, heads=heads),
        grid=(batch, GLA_HEADS // heads, nb),
        in_specs=[pl.BlockSpec((rows, wk), lambda b, g, c: (row(b, g, c), qb + g)),
                  pl.BlockSpec((rows, wk), lambda b, g, c: (row(b, g, c), kb + g)),
                  pl.BlockSpec((rows, wv), lambda b, g, c: (row(b, g, c), vb + g)),
                  pl.BlockSpec((rows, wv), lambda b, g, c: (row(b, g, c), ob + g)),
                  pl.BlockSpec((rows, V7X_LANES), lambda b, g, c: (row(b, g, c), 0)),
                  pl.BlockSpec((V7X_LANES, wk), lambda b, g, c: (0, g)),
                  pl.BlockSpec((1, wk), lambda b, g, c: (0, g)),
                  pl.BlockSpec((1, wv), lambda b, g, c: (0, g)),
                  pl.BlockSpec(sums.shape, lambda b, g, c: (0, 0)),
                  pl.BlockSpec(level_of.shape, lambda b, g, c: (0, 0))],
        out_specs=pl.BlockSpec((rows, wv), lambda b, g, c: (row(b, g, c), g)),
        out_shape=jax.ShapeDtypeStruct((t, GLA_HEADS * GLA_DV), BF16),
        scratch_shapes=[pltpu.VMEM((heads, GLA_DV, GLA_DK), F32), pltpu.VMEM((rows, wk), F32)],
        compiler_params=_params(3, blocks, scratch + (8 << 20)),
        name="gla_branch",
    )(z, z, z, z, a_lr, wup, b_a.reshape(1, -1), onorm_g.reshape(1, -1),
      jnp.asarray(sums, BF16), jnp.asarray(level_of))


def _layer(x, p_bf, lw, *, batch, seq, moe):
    t, d = x.shape
    cdim = lw["conv_dw"].shape[1]
    kdim, vdim = GLA_HEADS * GLA_DK, GLA_HEADS * GLA_DV
    tm, tn = 1024, 512

    h = rmsnorm(x, lw["norm_mix"], BF16)
    w_in = lw["w_in"]
    o_q = 2 * cdim
    o_alr = o_q + 2 * kdim + 2 * vdim
    o_gate = o_alr + GLA_RANK
    w_glu = w_in[:, :o_q].astype(BF16)
    w_z = jnp.concatenate([w_in[:, o_q:o_alr], w_in[:, o_gate:]], axis=1).astype(BF16)
    w_alr = jnp.zeros((d, V7X_LANES), BF16).at[:, :GLA_RANK].set(w_in[:, o_alr:o_gate].astype(BF16))

    u = fused_matmul([h], [(w_glu, *_col_tile(d, tn)), (w_glu, *_col_tile(d, tn, cdim // tn))], [],
                     _ep_glu, n_out=cdim, tm=tm, tn=tn, out_dtype=F32, name="in_glu", pair_a=(0, 0))
    n_z = w_z.shape[1]
    c_og = 2 * kdim + vdim
    c_ga = c_og + vdim
    z = fused_matmul([h], [(w_z, *_col_tile(d, tn))], [],
                     functools.partial(_ep_inproj, silu_from=c_og // tn, sigmoid_from=c_ga // tn),
                     n_out=n_z, tm=tm, tn=tn, out_dtype=BF16, name="in_z")
    a_lr = fused_matmul([h], [(w_alr, *_col_tile(d, V7X_LANES))], [], _ep_plain,
                        n_out=V7X_LANES, tm=tm, tn=V7X_LANES, out_dtype=F32, name="in_alr")

    y_conv = conv_branch(u, lw["conv_dw"], lw["conv_dw_b"], lw["conv_ln_g"], lw["conv_ln_b"], seq=seq)
    y_gla = gla_branch(z, a_lr, lw["gla_a_up"], lw["gla_a_b"], lw["gla_onorm"], batch=batch, seq=seq,
                       q_col=0, k_col=kdim, v_col=2 * kdim, og_col=c_og)

    def z_tile(off):
        return (tm, tn), lambda i, j: (i, j + off // tn)

    mix = fused_matmul([y_conv, y_gla],
                       [(lw["conv_w_out"].astype(BF16), *_col_tile(cdim, tn)),
                        (lw["gla_w_out"].astype(BF16), *_col_tile(vdim, tn))],
                       [(z, *z_tile(c_ga)), (z, *z_tile(c_ga + d))],
                       _ep_gated_sum, n_out=d, tm=tm, tn=tn, out_dtype=BF16, name="branch_out")

    def x_tile():
        return (tm, tn), lambda i, j: (i, j)

    x = fused_matmul([mix], [(lw["w_o"].astype(BF16), *_col_tile(d, tn))], [(x, *x_tile())],
                     _ep_residual, n_out=d, tm=tm, tn=tn, out_dtype=F32, name="w_o")

    if not moe:
        h = rmsnorm(x, lw["norm_ffn"], BF16)
        f = lw["ffn_w_gate"].shape[1]
        act = fused_matmul([h], [(lw["ffn_w_gate"].astype(BF16), *_col_tile(d, tn)),
                                 (lw["ffn_w_up"].astype(BF16), *_col_tile(d, tn))], [],
                           _ep_swiglu, n_out=f, tm=tm, tn=tn, out_dtype=BF16, name="ffn_up", pair_a=(0, 0))
        x = fused_matmul([act], [(lw["ffn_w_down"].astype(BF16), *_col_tile(f, tn))], [(x, *x_tile())],
                         _ep_residual, n_out=d, tm=tm, tn=tn, out_dtype=F32, name="ffn_down")
    else:
        h, comb = rmsnorm_router(x, lw["norm_ffn"], lw["router_w"], lw["router_b"])
        ne, _, fe = lw["moe_w_gate"].shape
        tpe = fe // tn

        def expert_tile():
            return (None, d, tn), lambda i, j: (j // tpe, 0, j % tpe)

        act = fused_matmul([h], [(lw["moe_w_gate"].astype(BF16), *expert_tile()),
                                 (lw["moe_w_up"].astype(BF16), *expert_tile())],
                           [(comb, (tm, V7X_LANES), lambda i, j: (i, 0))],
                           functools.partial(_ep_moe_up, tiles_per_expert=tpe),
                           n_out=ne * fe, tm=tm, tn=tn, out_dtype=BF16, name="moe_up", pair_a=(0, 0))
        tmd = 512
        x = fused_matmul([act], [(lw["moe_w_down"].astype(BF16).reshape(ne * fe, d), *_col_tile(ne * fe, tn))],
                         [(x, (tmd, tn), lambda i, j: (i, j))],
                         _ep_residual, n_out=d, tm=tmd, tn=tn, out_dtype=F32, name="moe_down")

    h = rmsnorm(x, lw["ple_norm"], BF16)
    pd = lw["ple_gate_down"].shape[1]
    g1 = fused_matmul([h], [(lw["ple_gate_down"].astype(BF16), *_col_tile(d, pd))], [], _ep_plain,
                      n_out=pd, tm=tm, tn=pd, out_dtype=BF16, name="ple_down")
    x = fused_matmul([p_bf, g1], [(lw["ple_proj"].astype(BF16), *_col_tile(pd, tn)),
                                  (lw["ple_gate_up"].astype(BF16), *_col_tile(pd, tn))],
                     [(x, *x_tile())], _ep_ple, n_out=d, tm=tm, tn=tn, out_dtype=F32, name="ple_up")
    return x


def kernel(x, p, norm_mix, w_in, conv_dw, conv_dw_b, conv_ln_g, conv_ln_b, conv_w_out, gla_a_up, gla_a_b, gla_onorm, gla_w_out, w_o, norm_ffn, ffn_w_gate, ffn_w_up, ffn_w_down, router_w, router_b, moe_w_gate, moe_w_up, moe_w_down, ple_norm, ple_gate_down, ple_gate_up, ple_proj, final_norm):
    batch, seq, d = x.shape
    depth = w_in.shape[0]
    xt = x.reshape(batch * seq, d)
    for i in range(depth):
        j = i // 2
        lw = dict(norm_mix=norm_mix[i], w_in=w_in[i], conv_dw=conv_dw[i], conv_dw_b=conv_dw_b[i],
                  conv_ln_g=conv_ln_g[i], conv_ln_b=conv_ln_b[i], conv_w_out=conv_w_out[i],
                  gla_a_up=gla_a_up[i], gla_a_b=gla_a_b[i], gla_onorm=gla_onorm[i], gla_w_out=gla_w_out[i],
                  w_o=w_o[i], norm_ffn=norm_ffn[i], ple_norm=ple_norm[i], ple_gate_down=ple_gate_down[i],
                  ple_gate_up=ple_gate_up[i], ple_proj=ple_proj[i])
        if i % 2 == 0:
            lw.update(ffn_w_gate=ffn_w_gate[j], ffn_w_up=ffn_w_up[j], ffn_w_down=ffn_w_down[j])
        else:
            lw.update(router_w=router_w[j], router_b=router_b[j], moe_w_gate=moe_w_gate[j],
                      moe_w_up=moe_w_up[j], moe_w_down=moe_w_down[j])
        p_bf = p[i].reshape(batch * seq, -1).astype(BF16)
        xt = _layer(xt, p_bf, lw, batch=batch, seq=seq, moe=(i % 2 == 1))
    return rmsnorm(xt, final_norm, F32).reshape(batch, seq, d)
```

```python
import functools

import numpy as np
import jax
import jax.numpy as jnp
from jax import lax
from jax.experimental import pallas as pl
from jax.experimental.pallas import tpu as pltpu

F32 = jnp.float32
BF16 = jnp.bfloat16

EPS = 1e-6
CONV_WIDTH = 31
CONV_HALO = 32
CONV_NORM_ROWS = 64
CONV_ROWS = 32
CONV_COLS = 512
GLA_HEADS = 8
GLA_DK = 128
GLA_DV = 256
GLA_RANK = 16
GLA_TAU = 16.0
GLA_CHUNK = 64
GLA_LEVELS = 6
N_EXPERTS = 8

V7X_LANES = 128
V7X_SCOPED_VMEM_CAP = 60000 * 1024


def _vmem_limit(block_bytes, temp_bytes=0):
    need = 2 * sum(block_bytes) + temp_bytes
    return int(min(V7X_SCOPED_VMEM_CAP, need + need // 4 + (4 << 20)))


def _nbytes(shape, dtype):
    return int(np.prod(shape)) * jnp.dtype(dtype).itemsize


def _params(n_grid, block_bytes, temp_bytes=0):
    return pltpu.CompilerParams(
        dimension_semantics=("arbitrary",) * n_grid,
        vmem_limit_bytes=_vmem_limit(block_bytes, temp_bytes))


def _rms_kernel(x_ref, g_ref, o_ref):
    x = x_ref[...]
    ms = jnp.mean(x * x, axis=-1, keepdims=True)
    o_ref[...] = (x * lax.rsqrt(ms + EPS) * g_ref[...]).astype(o_ref.dtype)


def rmsnorm(x, g, out_dtype, tm=256):
    t, d = x.shape
    blocks = [_nbytes((tm, d), F32), _nbytes((tm, d), out_dtype)]
    return pl.pallas_call(
        _rms_kernel,
        grid=(t // tm,),
        in_specs=[pl.BlockSpec((tm, d), lambda i: (i, 0)),
                  pl.BlockSpec((1, d), lambda i: (0, 0))],
        out_specs=pl.BlockSpec((tm, d), lambda i: (i, 0)),
        out_shape=jax.ShapeDtypeStruct((t, d), out_dtype),
        compiler_params=_params(1, blocks, 3 * _nbytes((tm, d), F32)),
        name="rmsnorm",
    )(x, g.reshape(1, d))


def _mm_kernel(*refs, pair_a, n_a, n_t, epilogue, col_axis, cast_w):
    n_w = len(pair_a)
    a_refs = refs[:n_a]
    w_refs = refs[n_a:n_a + n_w]
    t_refs = refs[n_a + n_w:n_a + n_w + n_t]
    o_ref = refs[n_a + n_w + n_t]
    if cast_w:
        wb_refs = refs[n_a + n_w + n_t + 1:]

        @pl.when(pl.program_id(1) == 0)
        def _():
            for w_ref, wb_ref in zip(w_refs, wb_refs):
                wb_ref[...] = w_ref[...].astype(BF16)

        w_refs = wb_refs
    accs = [jnp.dot(a_refs[ai][...], w_ref[...], preferred_element_type=F32)
            for ai, w_ref in zip(pair_a, w_refs)]
    tiles = [t_ref[...] for t_ref in t_refs]
    o_ref[...] = epilogue(accs, tiles, pl.program_id(col_axis)).astype(o_ref.dtype)


def fused_matmul(a_list, w_list, t_list, epilogue, *, n_out, tm, tn, out_dtype, name, pair_a=None,
                 w_resident=False):
    m = a_list[0].shape[0]
    pair_a = tuple(range(len(w_list))) if pair_a is None else tuple(pair_a)

    def order(im):
        return (lambda j, i: im(i, j)) if w_resident else im

    in_specs = [pl.BlockSpec((tm, a.shape[1]), order(lambda i, j: (i, 0))) for a in a_list]
    in_specs += [pl.BlockSpec(bs, order(im)) for _, bs, im in w_list]
    in_specs += [pl.BlockSpec(bs, order(im)) for _, bs, im in t_list]
    w_shapes = [tuple(s for s in bs if s is not None) for _, bs, _ in w_list]
    blocks = [_nbytes((tm, a.shape[1]), a.dtype) for a in a_list]
    blocks += [_nbytes(ws, w.dtype) for (w, _, _), ws in zip(w_list, w_shapes)]
    blocks += [_nbytes(bs, t.dtype) for t, bs, _ in t_list]
    blocks += [_nbytes((tm, tn), out_dtype)]
    temps = (len(w_list) + 1) * _nbytes((tm, tn), F32)
    scratch = [pltpu.VMEM(ws, BF16) for ws in w_shapes] if w_resident else []
    temps += sum(_nbytes(ws, BF16) for ws in w_shapes) if w_resident else 0
    grid = (n_out // tn, m // tm) if w_resident else (m // tm, n_out // tn)
    return pl.pallas_call(
        functools.partial(_mm_kernel, pair_a=pair_a, n_a=len(a_list), n_t=len(t_list), epilogue=epilogue,
                          col_axis=0 if w_resident else 1, cast_w=w_resident),
        grid=grid,
        in_specs=in_specs,
        out_specs=pl.BlockSpec((tm, tn), order(lambda i, j: (i, j))),
        out_shape=jax.ShapeDtypeStruct((m, n_out), out_dtype),
        scratch_shapes=scratch,
        compiler_params=_params(2, blocks, temps),
        name=name,
    )(*a_list, *[w for w, _, _ in w_list], *[t for t, _, _ in t_list])


def _layer_col_tile(layer, k, tn, off=0):
    return (None, k, tn), lambda i, j: (layer, 0, j + off)


def _col_tile(k, tn, off=0):
    return (k, tn), lambda i, j: (0, j + off)


def _ep_plain(accs, tiles, j):
    return accs[0]


def _ep_glu(accs, tiles, j):
    return accs[0] * jax.nn.sigmoid(accs[1])


def _ep_swiglu(accs, tiles, j):
    return jax.nn.silu(accs[0]) * accs[1]


def _ep_silu_from(accs, tiles, j, *, silu_from):
    z = accs[0]
    return jnp.where(j >= silu_from, z * jax.nn.sigmoid(z), z)


def _ep_sigmoid(accs, tiles, j):
    return jax.nn.sigmoid(accs[0])


def _ep_gated_sum(accs, tiles, j):
    return accs[0] * tiles[0].astype(F32) + accs[1] * tiles[1].astype(F32)


def _ep_residual(accs, tiles, j):
    return tiles[0] + accs[0]


def _ep_ple(accs, tiles, j):
    return tiles[0] + accs[0] * jax.nn.sigmoid(accs[1])


def _conv_kernel(u_ref, halo_ref, w_ref, b_ref, g_ref, beta_ref, o_ref, xs_ref, y_ref, *, tm, seq):
    c = u_ref.shape[1]
    ext = tm + CONV_HALO - 8
    first = (pl.program_id(0) * tm) % seq == 0
    xs_ref[0, 0:CONV_HALO, :] = jnp.where(first, 0.0, halo_ref[...])
    xs_ref[0, CONV_HALO:CONV_HALO + tm, :] = u_ref[...]
    for s in range(1, 8):
        for c0 in range(0, c, V7X_LANES):
            xs_ref[s, 0:ext, c0:c0 + V7X_LANES] = xs_ref[0, s:s + ext, c0:c0 + V7X_LANES]

    lead = CONV_HALO - (CONV_WIDTH - 1)

    def conv_rows(r, carry):
        r0 = pl.multiple_of(r * CONV_ROWS, CONV_ROWS)
        for c0 in range(0, c, CONV_COLS):
            cs = slice(c0, c0 + CONV_COLS)
            accs = [b_ref[:, cs]] * (CONV_ROWS // 8)
            for w in range(CONV_WIDTH):
                off = lead + w
                base = r0 + (off // 8) * 8
                wv = w_ref[8 * w:8 * w + 8, cs]
                accs = [a + xs_ref[off % 8, pl.ds(base + 8 * g, 8), cs] * wv for g, a in enumerate(accs)]
            for g, a in enumerate(accs):
                y_ref[pl.ds(r0 + 8 * g, 8), cs] = a
        return carry

    lax.fori_loop(0, tm // CONV_ROWS, conv_rows, 0)

    def norm_rows(r, carry):
        r0 = pl.multiple_of(r * CONV_NORM_ROWS, CONV_NORM_ROWS)
        y = y_ref[pl.ds(r0, CONV_NORM_ROWS), :]
        mu = jnp.mean(y, axis=-1, keepdims=True)
        d = y - mu
        var = jnp.mean(d * d, axis=-1, keepdims=True)
        z = d * lax.rsqrt(var + EPS) * g_ref[...] + beta_ref[...]
        o_ref[pl.ds(r0, CONV_NORM_ROWS), :] = jax.nn.silu(z).astype(o_ref.dtype)
        return carry

    lax.fori_loop(0, tm // CONV_NORM_ROWS, norm_rows, 0)


def conv_branch(u, w_dw, b_dw, ln_g, ln_b, *, seq, tm=256):
    t, c = u.shape
    assert seq % tm == 0 and tm % CONV_HALO == 0 and tm % CONV_NORM_ROWS == 0
    hb = tm // CONV_HALO
    w_rep = jnp.repeat(w_dw, 8, axis=0)
    b_rep = jnp.broadcast_to(b_dw.reshape(1, c), (8, c))
    blocks = [_nbytes((tm, c), F32), _nbytes((CONV_HALO, c), F32), _nbytes(w_rep.shape, F32),
              _nbytes((tm, c), BF16)]
    scratch = 8 * _nbytes((tm + CONV_HALO, c), F32) + _nbytes((tm, c), F32)
    return pl.pallas_call(
        functools.partial(_conv_kernel, tm=tm, seq=seq),
        grid=(t // tm,),
        in_specs=[pl.BlockSpec((tm, c), lambda i: (i, 0)),
                  pl.BlockSpec((CONV_HALO, c), lambda i: (jnp.maximum(i * hb - 1, 0), 0)),
                  pl.BlockSpec(w_rep.shape, lambda i: (0, 0)),
                  pl.BlockSpec((8, c), lambda i: (0, 0)),
                  pl.BlockSpec((1, c), lambda i: (0, 0)),
                  pl.BlockSpec((1, c), lambda i: (0, 0))],
        out_specs=pl.BlockSpec((tm, c), lambda i: (i, 0)),
        out_shape=jax.ShapeDtypeStruct((t, c), BF16),
        scratch_shapes=[pltpu.VMEM((8, tm + CONV_HALO, c), F32), pltpu.VMEM((tm, c), F32)],
        compiler_params=_params(1, blocks, scratch),
        name="conv_branch",
    )(u, u, w_rep, b_rep, ln_g.reshape(1, c), ln_b.reshape(1, c))


def _gla_constants():
    cn = GLA_CHUNK
    r = np.arange(cn)[:, None]
    c = np.arange(cn)[None, :]
    parts = [c <= r, c > r]
    for lv in range(GLA_LEVELS):
        half = 1 << lv
        mid = (r // (2 * half)) * (2 * half) + half - 1
        right = (r % (2 * half)) >= half
        parts.append(np.where(right, (c > mid) & (c <= r), (c > r) & (c <= mid)))
    parts.append(np.ones((16, cn), bool))
    sums = np.concatenate(parts).astype(np.float32)
    high_bit = np.floor(np.log2(np.maximum(r ^ c, 1))).astype(np.int32)
    level_of = np.where(c < r, high_bit, np.where(c == r, GLA_LEVELS, -1)).astype(np.int32)
    return sums, level_of


def _gla_kernel(q_ref, k_ref, v_ref, og_ref, alr_ref, wup_ref, bup_ref, gn_ref,
                sums_ref, level_ref, o_ref, st_ref, la_ref, *, n_chunks, heads):
    cn = GLA_CHUNK

    @pl.when(pl.program_id(2) == 0)
    def _():
        st_ref[...] = jnp.zeros_like(st_ref)

    x = jnp.dot(alr_ref[...].astype(BF16), wup_ref[...], preferred_element_type=F32) + bup_ref[...]
    la_ref[...] = (jnp.minimum(x, 0.0) - jnp.log(1.0 + jnp.exp(-jnp.abs(x)))) * (1.0 / GLA_TAU)

    ones_dk = jnp.ones((GLA_DK, cn), BF16)
    level_of = level_ref[...]
    nt = (((1,), (1,)), ((), ()))
    tn = (((0,), (0,)), ((), ()))

    def chunk(ci, carry):
        r0 = pl.multiple_of(ci * cn, cn)
        la = la_ref[pl.ds(r0, cn), :]
        hi = la.astype(BF16)
        lo = (la - hi.astype(F32)).astype(BF16)
        sums = sums_ref[...]
        ex_all = jnp.exp(jnp.dot(sums, hi, preferred_element_type=F32)
                         + jnp.dot(sums, lo, preferred_element_type=F32))
        for h in range(heads):
            kc = slice(h * GLA_DK, (h + 1) * GLA_DK)
            vc = slice(h * GLA_DV, (h + 1) * GLA_DV)
            ex = ex_all[:, kc]
            q = q_ref[pl.ds(r0, cn), kc].astype(F32) * (GLA_DK ** -0.5)
            k = k_ref[pl.ds(r0, cn), kc].astype(F32)
            v = v_ref[pl.ds(r0, cn), vc]
            q_state = q * ex[0:cn]
            k_dec = k * ex[cn:2 * cn]
            e_last = ex[(2 + GLA_LEVELS) * cn:(2 + GLA_LEVELS) * cn + 1]

            diag = jnp.dot((q * k).astype(BF16), ones_dk, preferred_element_type=F32)
            scores = jnp.where(level_of == GLA_LEVELS, diag, 0.0)
            for lv in range(GLA_LEVELS):
                e = ex[(2 + lv) * cn:(3 + lv) * cn]
                s = lax.dot_general((q * e).astype(BF16), (k * e).astype(BF16), nt,
                                    preferred_element_type=F32)
                scores = jnp.where(level_of == lv, s, scores)

            st = st_ref[h]
            o = (jnp.dot(scores.astype(BF16), v, preferred_element_type=F32)
                 + lax.dot_general(q_state.astype(BF16), st.astype(BF16), nt, preferred_element_type=F32))
            st_ref[h] = st * e_last + lax.dot_general(v, k_dec.astype(BF16), tn, preferred_element_type=F32)

            o = o * lax.rsqrt(jnp.mean(o * o, axis=-1, keepdims=True) + EPS) * gn_ref[:, vc]
            o_ref[pl.ds(r0, cn), vc] = (o * og_ref[pl.ds(r0, cn), vc].astype(F32)).astype(o_ref.dtype)
        return carry

    lax.fori_loop(0, n_chunks, chunk, 0)


def gla_branch(z, a_lr, w_a_up, b_a, onorm_g, *, batch, seq, q_col, k_col, v_col, og_col, rows=1024, heads=8):
    t = z.shape[0]
    assert seq % rows == 0 and rows % GLA_CHUNK == 0 and GLA_HEADS % heads == 0
    nb = seq // rows
    wk, wv = heads * GLA_DK, heads * GLA_DV
    sums, level_of = _gla_constants()
    wup = jnp.zeros((V7X_LANES, GLA_HEADS * GLA_DK), F32).at[:GLA_RANK].set(w_a_up).astype(BF16)
    qb, kb, vb, ob = q_col // wk, k_col // wk, v_col // wv, og_col // wv

    def row(b, g, c):
        return b * nb + c

    blocks = [2 * _nbytes((rows, wk), BF16), 3 * _nbytes((rows, wv), BF16),
              _nbytes((rows, V7X_LANES), F32), _nbytes(sums.shape, BF16) * 2]
    scratch = heads * _nbytes((GLA_DV, GLA_DK), F32) + _nbytes((rows, wk), F32)
    return pl.pallas_call(
        functools.partial(_gla_kernel, n_chunks=rows // GLA_CHUNK, heads=heads),
        grid=(batch, GLA_HEADS // heads, nb),
        in_specs=[pl.BlockSpec((rows, wk), lambda b, g, c: (row(b, g, c), qb + g)),
                  pl.BlockSpec((rows, wk), lambda b, g, c: (row(b, g, c), kb + g)),
                  pl.BlockSpec((rows, wv), lambda b, g, c: (row(b, g, c), vb + g)),
                  pl.BlockSpec((rows, wv), lambda b, g, c: (row(b, g, c), ob + g)),
                  pl.BlockSpec((rows, V7X_LANES), lambda b, g, c: (row(b, g, c), 0)),
                  pl.BlockSpec((V7X_LANES, wk), lambda b, g, c: (0, g)),
                  pl.BlockSpec((1, wk), lambda b, g, c: (0, g)),
                  pl.BlockSpec((1, wv), lambda b, g, c: (0, g)),
                  pl.BlockSpec(sums.shape, lambda b, g, c: (0, 0)),
                  pl.BlockSpec(level_of.shape, lambda b, g, c: (0, 0))],
        out_specs=pl.BlockSpec((rows, wv), lambda b, g, c: (row(b, g, c), g)),
        out_shape=jax.ShapeDtypeStruct((t, GLA_HEADS * GLA_DV), BF16),
        scratch_shapes=[pltpu.VMEM((heads, GLA_DV, GLA_DK), F32), pltpu.VMEM((rows, wk), F32)],
        compiler_params=_params(3, blocks, scratch + (8 << 20)),
        name="gla_branch",
    )(z, z, z, z, a_lr, wup, b_a.reshape(1, -1), onorm_g.reshape(1, -1),
      jnp.asarray(sums, BF16), jnp.asarray(level_of))


MOE_TILE = 512
MOE_GATHER_ROWS = 256


def _route_kernel(x_ref, g_ref, wr_ref, br_ref, sel_ref, wts_ref):
    x = x_ref[...]
    ms = jnp.mean(x * x, axis=-1, keepdims=True)
    h = x * lax.rsqrt(ms + EPS) * g_ref[...]
    logits = jnp.dot(h, wr_ref[...], preferred_element_type=F32,
                     precision=lax.Precision.HIGHEST) + br_ref[...]
    lane = lax.broadcasted_iota(jnp.int32, logits.shape, 1)
    logits = jnp.where(lane < N_EXPERTS, logits, -jnp.inf)
    m1 = jnp.max(logits, axis=-1, keepdims=True)
    i1 = jnp.min(jnp.where(logits == m1, lane, V7X_LANES), axis=-1, keepdims=True)
    rest = jnp.where(lane == i1, -jnp.inf, logits)
    m2 = jnp.max(rest, axis=-1, keepdims=True)
    i2 = jnp.min(jnp.where(rest == m2, lane, V7X_LANES), axis=-1, keepdims=True)
    e2 = jnp.exp(m2 - m1)
    w1 = 1.0 / (1.0 + e2)
    w2 = e2 / (1.0 + e2)
    sel_ref[...] = jnp.where(lane == 0, i1, jnp.where(lane == 1, i2, 0))
    wts_ref[...] = jnp.where(lane == 0, w1, jnp.where(lane == 1, w2, 0.0))


def moe_route(x, g, router_w, router_b, tm=256):
    t, d = x.shape
    wr = jnp.zeros((d, V7X_LANES), F32).at[:, :N_EXPERTS].set(router_w)
    br = jnp.zeros((1, V7X_LANES), F32).at[0, :N_EXPERTS].set(router_b)
    blocks = [_nbytes((tm, d), F32), _nbytes((d, V7X_LANES), F32)]
    return pl.pallas_call(
        _route_kernel,
        grid=(t // tm,),
        in_specs=[pl.BlockSpec((tm, d), lambda i: (i, 0)),
                  pl.BlockSpec((1, d), lambda i: (0, 0)),
                  pl.BlockSpec((d, V7X_LANES), lambda i: (0, 0)),
                  pl.BlockSpec((1, V7X_LANES), lambda i: (0, 0))],
        out_specs=[pl.BlockSpec((tm, V7X_LANES), lambda i: (i, 0)),
                   pl.BlockSpec((tm, V7X_LANES), lambda i: (i, 0))],
        out_shape=[jax.ShapeDtypeStruct((t, V7X_LANES), jnp.int32),
                   jax.ShapeDtypeStruct((t, V7X_LANES), F32)],
        compiler_params=_params(1, blocks, 3 * _nbytes((tm, d), F32)),
        name="moe_route",
    )(x, g.reshape(1, d), wr, br)


def _moe_plan(sel, wts):
    t = sel.shape[0]
    na = 2 * t
    e = sel[:, :2].reshape(na)
    w = wts[:, :2].reshape(na)
    ids = jnp.arange(N_EXPERTS, dtype=jnp.int32)
    order = jnp.argsort(e, stable=True).astype(jnp.int32)
    counts = jnp.sum((e[:, None] == ids[None, :]).astype(jnp.int32), axis=0)
    padded = (counts + MOE_TILE - 1) // MOE_TILE * MOE_TILE
    pend = jnp.cumsum(padded)
    pstart = pend - padded
    cstart = jnp.cumsum(counts) - counts
    n_rows = na + N_EXPERTS * MOE_TILE
    r = jnp.arange(n_rows, dtype=jnp.int32)
    e_r = jnp.minimum(jnp.sum((r[:, None] >= pend[None, :]).astype(jnp.int32), axis=1), N_EXPERTS - 1)
    local = r - pstart[e_r]
    valid = local < counts[e_r]
    a_r = order[jnp.clip(cstart[e_r] + local, 0, na - 1)]
    src = jnp.where(valid, a_r // 2, 0)
    w_row = jnp.where(valid, w[a_r], 0.0)
    inv = jnp.zeros((na,), jnp.int32).at[order].set(jnp.arange(na, dtype=jnp.int32))
    pos = (inv + (pstart - cstart)[e]).reshape(t, 2)
    tile_expert = e_r[::MOE_TILE]
    tile_used = (r[::MOE_TILE] < pend[N_EXPERTS - 1]).astype(jnp.int32)
    return src, w_row, pos, tile_expert, tile_used


def _start_row_gather(idx_ref, src_hbm, dst_ref, sem, n):
    def body(r, carry):
        pltpu.make_async_copy(src_hbm.at[pl.ds(idx_ref[0, 0, r], 1)], dst_ref.at[pl.ds(r, 1)], sem).start()
        return carry

    lax.fori_loop(0, n, body, 0)


def _wait_row_gather(src_hbm, dst_ref, sem, n):
    pltpu.make_async_copy(src_hbm.at[pl.ds(0, n)], dst_ref.at[pl.ds(0, n)], sem).wait()


def _pipelined_row_gather(idx_ref, idx_next_ref, src_hbm, buf_ref, sem_ref, n):
    i = pl.program_id(0)
    slot = i % 2

    @pl.when(i == 0)
    def _():
        _start_row_gather(idx_ref, src_hbm, buf_ref.at[0], sem_ref.at[0], n)

    @pl.when(i + 1 < pl.num_programs(0))
    def _():
        _start_row_gather(idx_next_ref, src_hbm, buf_ref.at[1 - slot], sem_ref.at[1 - slot], n)

    _wait_row_gather(src_hbm, buf_ref.at[slot], sem_ref.at[slot], n)
    return slot


def _gather_norm_kernel(idx_ref, idx_next_ref, x_hbm, g_ref, o_ref, buf_ref, sem_ref):
    slot = _pipelined_row_gather(idx_ref, idx_next_ref, x_hbm, buf_ref, sem_ref, o_ref.shape[0])
    x = buf_ref[slot]
    ms = jnp.mean(x * x, axis=-1, keepdims=True)
    o_ref[...] = (x * lax.rsqrt(ms + EPS) * g_ref[...]).astype(o_ref.dtype)


def _row_index_specs(idx, rows):
    n = idx.shape[0] // rows
    idx3 = idx.reshape(n, 1, rows)
    cur = pl.BlockSpec((1, 1, rows), lambda i: (i, 0, 0), memory_space=pltpu.SMEM)
    nxt = pl.BlockSpec((1, 1, rows), lambda i: (jnp.minimum(i + 1, n - 1), 0, 0), memory_space=pltpu.SMEM)
    return idx3, cur, nxt


def moe_gather_norm(x, g, src):
    t, d = x.shape
    rows = MOE_GATHER_ROWS
    n_rows = src.shape[0]
    idx3, cur, nxt = _row_index_specs(src, rows)
    blocks = [_nbytes((rows, d), BF16)]
    scratch = 2 * _nbytes((rows, d), F32)
    return pl.pallas_call(
        _gather_norm_kernel,
        grid=(n_rows // rows,),
        in_specs=[cur, nxt, pl.BlockSpec(memory_space=pl.ANY), pl.BlockSpec((1, d), lambda i: (0, 0))],
        out_specs=pl.BlockSpec((rows, d), lambda i: (i, 0)),
        out_shape=jax.ShapeDtypeStruct((n_rows, d), BF16),
        scratch_shapes=[pltpu.VMEM((2, rows, d), F32), pltpu.SemaphoreType.DMA((2,))],
        compiler_params=_params(1, blocks, scratch + 3 * _nbytes((rows, d), F32)),
        name="moe_gather_norm",
    )(idx3, idx3, x, g.reshape(1, d))


def _expert_kernel(te_ref, tu_ref, h_ref, wrow_ref, wg_ref, wu_ref, wd_ref, o_ref, acc_ref):
    j = pl.program_id(1)

    @pl.when(j == 0)
    def _():
        acc_ref[...] = jnp.zeros_like(acc_ref)

    @pl.when(tu_ref[pl.program_id(0)] != 0)
    def _():
        h = h_ref[...]
        a = (jax.nn.silu(jnp.dot(h, wg_ref[...], preferred_element_type=F32))
             * jnp.dot(h, wu_ref[...], preferred_element_type=F32) * wrow_ref[...])
        acc_ref[...] += jnp.dot(a.astype(BF16), wd_ref[...], preferred_element_type=F32)

    @pl.when(j == pl.num_programs(1) - 1)
    def _():
        o_ref[...] = acc_ref[...]


def moe_experts(h_sorted, w_row, tile_expert, tile_used, w_gate, w_up, w_down, th=256):
    n_rows, d = h_sorted.shape
    fe = w_gate.shape[2]
    tm = MOE_TILE
    nj = fe // th

    def hidden(i, j, te, tu):
        return jnp.where(tu[i] != 0, j, nj - 1)

    blocks = [_nbytes((tm, d), BF16), _nbytes((tm, V7X_LANES), F32), 3 * _nbytes((d, th), BF16),
              _nbytes((tm, d), F32)]
    return pl.pallas_call(
        _expert_kernel,
        grid_spec=pltpu.PrefetchScalarGridSpec(
            num_scalar_prefetch=2,
            grid=(n_rows // tm, nj),
            in_specs=[pl.BlockSpec((tm, d), lambda i, j, te, tu: (i, 0)),
                      pl.BlockSpec((tm, 1), lambda i, j, te, tu: (i, 0)),
                      pl.BlockSpec((None, d, th), lambda i, j, te, tu: (te[i], 0, hidden(i, j, te, tu))),
                      pl.BlockSpec((None, d, th), lambda i, j, te, tu: (te[i], 0, hidden(i, j, te, tu))),
                      pl.BlockSpec((None, th, d), lambda i, j, te, tu: (te[i], hidden(i, j, te, tu), 0))],
            out_specs=pl.BlockSpec((tm, d), lambda i, j, te, tu: (i, 0)),
            scratch_shapes=[pltpu.VMEM((tm, d), F32)]),
        out_shape=jax.ShapeDtypeStruct((n_rows, d), F32),
        compiler_params=_params(2, blocks, 2 * _nbytes((tm, d), F32)),
        name="moe_experts",
    )(tile_expert, tile_used, h_sorted, w_row.reshape(n_rows, 1), w_gate, w_up, w_down)


def _combine_kernel(idx_ref, idx_next_ref, y_hbm, x_ref, o_ref, buf_ref, sem_ref):
    tc = x_ref.shape[0]
    slot = _pipelined_row_gather(idx_ref, idx_next_ref, y_hbm, buf_ref, sem_ref, 2 * tc)
    o_ref[...] = x_ref[...] + buf_ref[slot, 0:tc, :] + buf_ref[slot, tc:2 * tc, :]


def moe_combine(x, y_sorted, pos):
    t, d = x.shape
    tc = MOE_GATHER_ROWS
    n = t // tc
    idx = pos.reshape(n, tc, 2).transpose(0, 2, 1).reshape(n * 2 * tc)
    idx3, cur, nxt = _row_index_specs(idx, 2 * tc)
    blocks = [2 * _nbytes((tc, d), F32)]
    scratch = 2 * _nbytes((2 * tc, d), F32)
    return pl.pallas_call(
        _combine_kernel,
        grid=(n,),
        in_specs=[cur, nxt, pl.BlockSpec(memory_space=pl.ANY), pl.BlockSpec((tc, d), lambda i: (i, 0))],
        out_specs=pl.BlockSpec((tc, d), lambda i: (i, 0)),
        out_shape=jax.ShapeDtypeStruct((t, d), F32),
        scratch_shapes=[pltpu.VMEM((2, 2 * tc, d), F32), pltpu.SemaphoreType.DMA((2,))],
        compiler_params=_params(1, blocks, scratch),
        name="moe_combine",
    )(idx3, idx3, y_sorted, x)


def _layer(x, p_bf, w, li, *, batch, seq):
    t, d = x.shape
    cdim = w["conv_dw"].shape[2]
    kdim, vdim = GLA_HEADS * GLA_DK, GLA_HEADS * GLA_DV
    tm, tn, tn2 = 1024, 512, 256
    fj = li // 2

    h = rmsnorm(x, w["norm_mix"][li], BF16)
    w_in = w["w_in"]
    o_q = 2 * cdim
    o_alr = o_q + 2 * kdim + 2 * vdim
    o_gate = o_alr + GLA_RANK
    c_og = 2 * kdim + vdim
    u = fused_matmul([h], [(w_in, *_layer_col_tile(li, d, tn2)), (w_in, *_layer_col_tile(li, d, tn2, cdim // tn2))],
                     [], _ep_glu, n_out=cdim, tm=tm, tn=tn2, out_dtype=F32, name="in_glu", pair_a=(0, 0),
                     w_resident=True)
    z = fused_matmul([h], [(w_in, *_layer_col_tile(li, d, tn, o_q // tn))], [],
                     functools.partial(_ep_silu_from, silu_from=c_og // tn),
                     n_out=o_alr - o_q, tm=tm, tn=tn, out_dtype=BF16, name="in_z", w_resident=True)
    gates = fused_matmul([h], [(w_in[li, :, o_gate:], *_col_tile(d, tn))], [], _ep_sigmoid,
                         n_out=2 * d, tm=tm, tn=tn, out_dtype=BF16, name="in_gate", w_resident=True)
    w_alr = jnp.zeros((d, V7X_LANES), F32).at[:, :GLA_RANK].set(w_in[li, :, o_alr:o_gate])
    a_lr = fused_matmul([h], [(w_alr, *_col_tile(d, V7X_LANES))], [], _ep_plain,
                        n_out=V7X_LANES, tm=tm, tn=V7X_LANES, out_dtype=F32, name="in_alr", w_resident=True)

    y_conv = conv_branch(u, w["conv_dw"][li], w["conv_dw_b"][li], w["conv_ln_g"][li], w["conv_ln_b"][li], seq=seq)
    y_gla = gla_branch(z, a_lr, w["gla_a_up"][li], w["gla_a_b"][li], w["gla_onorm"][li], batch=batch, seq=seq,
                       q_col=0, k_col=kdim, v_col=2 * kdim, og_col=c_og)

    def tile(off=0):
        return (tm, tn), lambda i, j: (i, j + off // tn)

    mix = fused_matmul([y_conv, y_gla],
                       [(w["conv_w_out"], *_layer_col_tile(li, cdim, tn)),
                        (w["gla_w_out"], *_layer_col_tile(li, vdim, tn))],
                       [(gates, *tile()), (gates, *tile(d))],
                       _ep_gated_sum, n_out=d, tm=tm, tn=tn, out_dtype=BF16, name="branch_out", w_resident=True)
    x = fused_matmul([mix], [(w["w_o"], *_layer_col_tile(li, d, tn))], [(x, *tile())],
                     _ep_residual, n_out=d, tm=tm, tn=tn, out_dtype=F32, name="w_o", w_resident=True)

    if li % 2 == 0:
        h = rmsnorm(x, w["norm_ffn"][li], BF16)
        f = w["ffn_w_gate"].shape[2]
        act = fused_matmul([h], [(w["ffn_w_gate"], *_layer_col_tile(fj, d, tn2)),
                                 (w["ffn_w_up"], *_layer_col_tile(fj, d, tn2))], [],
                           _ep_swiglu, n_out=f, tm=tm, tn=tn2, out_dtype=BF16, name="ffn_up", pair_a=(0, 0),
                           w_resident=True)
        x = fused_matmul([act], [(w["ffn_w_down"][fj].astype(BF16), *_col_tile(f, tn))], [(x, *tile())],
                         _ep_residual, n_out=d, tm=tm, tn=tn, out_dtype=F32, name="ffn_down")
    else:
        sel, wts = moe_route(x, w["norm_ffn"][li], w["router_w"][fj], w["router_b"][fj])
        src, w_row, pos, tile_expert, tile_used = _moe_plan(sel, wts)
        h_sorted = moe_gather_norm(x, w["norm_ffn"][li], src)
        y_sorted = moe_experts(h_sorted, w_row, tile_expert, tile_used, w["moe_w_gate"][fj].astype(BF16),
                               w["moe_w_up"][fj].astype(BF16), w["moe_w_down"][fj].astype(BF16))
        x = moe_combine(x, y_sorted, pos)

    h = rmsnorm(x, w["ple_norm"][li], BF16)
    pd = w["ple_gate_down"].shape[2]
    g1 = fused_matmul([h], [(w["ple_gate_down"][li].astype(BF16), *_col_tile(d, pd))], [], _ep_plain,
                      n_out=pd, tm=tm, tn=pd, out_dtype=BF16, name="ple_down")
    x = fused_matmul([p_bf, g1], [(w["ple_proj"][li].astype(BF16), *_col_tile(pd, tn)),
                                  (w["ple_gate_up"][li].astype(BF16), *_col_tile(pd, tn))],
                     [(x, *tile())], _ep_ple, n_out=d, tm=tm, tn=tn, out_dtype=F32, name="ple_up")
    return x


def kernel(x, p, norm_mix, w_in, conv_dw, conv_dw_b, conv_ln_g, conv_ln_b, conv_w_out, gla_a_up, gla_a_b, gla_onorm, gla_w_out, w_o, norm_ffn, ffn_w_gate, ffn_w_up, ffn_w_down, router_w, router_b, moe_w_gate, moe_w_up, moe_w_down, ple_norm, ple_gate_down, ple_gate_up, ple_proj, final_norm):
    batch, seq, d = x.shape
    w = dict(norm_mix=norm_mix, w_in=w_in, conv_dw=conv_dw, conv_dw_b=conv_dw_b, conv_ln_g=conv_ln_g,
             conv_ln_b=conv_ln_b, conv_w_out=conv_w_out, gla_a_up=gla_a_up, gla_a_b=gla_a_b, gla_onorm=gla_onorm,
             gla_w_out=gla_w_out, w_o=w_o, norm_ffn=norm_ffn, ffn_w_gate=ffn_w_gate, ffn_w_up=ffn_w_up,
             ffn_w_down=ffn_w_down, router_w=router_w, router_b=router_b, moe_w_gate=moe_w_gate,
             moe_w_up=moe_w_up, moe_w_down=moe_w_down, ple_norm=ple_norm, ple_gate_down=ple_gate_down,
             ple_gate_up=ple_gate_up, ple_proj=ple_proj)
    xt = x.reshape(batch * seq, d)
    for li in range(w_in.shape[0]):
        p_bf = p[li].reshape(batch * seq, -1).astype(BF16)
        xt = _layer(xt, p_bf, w, li, batch=batch, seq=seq)
    return rmsnorm(xt, final_norm, F32).reshape(batch, seq, d)
```

```python
import functools

import numpy as np
import jax
import jax.numpy as jnp
from jax import lax
from jax.experimental import pallas as pl
from jax.experimental.pallas import tpu as pltpu

F32 = jnp.float32
BF16 = jnp.bfloat16

EPS = 1e-6
CONV_WIDTH = 31
CONV_HALO = 32
CONV_NORM_ROWS = 64
CONV_ROWS = 32
CONV_COLS = 512
GLA_HEADS = 8
GLA_DK = 128
GLA_DV = 256
GLA_RANK = 16
GLA_TAU = 16.0
GLA_CHUNK = 64
GLA_LEVELS = 6
N_EXPERTS = 8

V7X_LANES = 128
V7X_SCOPED_VMEM_CAP = 60000 * 1024


def _vmem_limit(block_bytes, temp_bytes=0):
    need = 2 * sum(block_bytes) + temp_bytes
    return int(min(V7X_SCOPED_VMEM_CAP, need + need // 4 + (4 << 20)))


def _nbytes(shape, dtype):
    return int(np.prod(shape)) * jnp.dtype(dtype).itemsize


def _params(n_grid, block_bytes, temp_bytes=0):
    return pltpu.CompilerParams(
        dimension_semantics=("arbitrary",) * n_grid,
        vmem_limit_bytes=_vmem_limit(block_bytes, temp_bytes))


def _rms_kernel(x_ref, g_ref, o_ref):
    x = x_ref[...]
    ms = jnp.mean(x * x, axis=-1, keepdims=True)
    o_ref[...] = (x * lax.rsqrt(ms + EPS) * g_ref[...]).astype(o_ref.dtype)


def rmsnorm(x, g, out_dtype, tm=256):
    t, d = x.shape
    blocks = [_nbytes((tm, d), F32), _nbytes((tm, d), out_dtype)]
    return pl.pallas_call(
        _rms_kernel,
        grid=(t // tm,),
        in_specs=[pl.BlockSpec((tm, d), lambda i: (i, 0)),
                  pl.BlockSpec((1, d), lambda i: (0, 0))],
        out_specs=pl.BlockSpec((tm, d), lambda i: (i, 0)),
        out_shape=jax.ShapeDtypeStruct((t, d), out_dtype),
        compiler_params=_params(1, blocks, 3 * _nbytes((tm, d), F32)),
        name="rmsnorm",
    )(x, g.reshape(1, d))


def _mm_kernel(*refs, pair_a, n_a, n_t, epilogue, col_axis, cast_w):
    n_w = len(pair_a)
    a_refs = refs[:n_a]
    w_refs = refs[n_a:n_a + n_w]
    t_refs = refs[n_a + n_w:n_a + n_w + n_t]
    o_ref = refs[n_a + n_w + n_t]
    if cast_w:
        wb_refs = refs[n_a + n_w + n_t + 1:]

        @pl.when(pl.program_id(1) == 0)
        def _():
            for w_ref, wb_ref in zip(w_refs, wb_refs):
                wb_ref[...] = w_ref[...].astype(BF16)

        w_refs = wb_refs
    accs = [jnp.dot(a_refs[ai][...], w_ref[...], preferred_element_type=F32)
            for ai, w_ref in zip(pair_a, w_refs)]
    tiles = [t_ref[...] for t_ref in t_refs]
    o_ref[...] = epilogue(accs, tiles, pl.program_id(col_axis)).astype(o_ref.dtype)


def fused_matmul(a_list, w_list, t_list, epilogue, *, n_out, tm, tn, out_dtype, name, pair_a=None,
                 w_resident=False):
    m = a_list[0].shape[0]
    pair_a = tuple(range(len(w_list))) if pair_a is None else tuple(pair_a)
    cast_w = any(w.dtype != BF16 for w, _, _ in w_list)
    assert w_resident or not cast_w

    def order(im):
        return (lambda j, i: im(i, j)) if w_resident else im

    in_specs = [pl.BlockSpec((tm, a.shape[1]), order(lambda i, j: (i, 0))) for a in a_list]
    in_specs += [pl.BlockSpec(bs, order(im)) for _, bs, im in w_list]
    in_specs += [pl.BlockSpec(bs, order(im)) for _, bs, im in t_list]
    w_shapes = [tuple(s for s in bs if s is not None) for _, bs, _ in w_list]
    blocks = [_nbytes((tm, a.shape[1]), a.dtype) for a in a_list]
    blocks += [_nbytes(ws, w.dtype) for (w, _, _), ws in zip(w_list, w_shapes)]
    blocks += [_nbytes(bs, t.dtype) for t, bs, _ in t_list]
    blocks += [_nbytes((tm, tn), out_dtype)]
    temps = (len(w_list) + 1) * _nbytes((tm, tn), F32)
    scratch = [pltpu.VMEM(ws, BF16) for ws in w_shapes] if cast_w else []
    temps += sum(_nbytes(ws, BF16) for ws in w_shapes) if cast_w else 0
    grid = (n_out // tn, m // tm) if w_resident else (m // tm, n_out // tn)
    return pl.pallas_call(
        functools.partial(_mm_kernel, pair_a=pair_a, n_a=len(a_list), n_t=len(t_list), epilogue=epilogue,
                          col_axis=0 if w_resident else 1, cast_w=cast_w),
        grid=grid,
        in_specs=in_specs,
        out_specs=pl.BlockSpec((tm, tn), order(lambda i, j: (i, j))),
        out_shape=jax.ShapeDtypeStruct((m, n_out), out_dtype),
        scratch_shapes=scratch,
        compiler_params=_params(2, blocks, temps),
        name=name,
    )(*a_list, *[w for w, _, _ in w_list], *[t for t, _, _ in t_list])


def _layer_col_tile(layer, k, tn, off=0):
    return (None, k, tn), lambda i, j: (layer, 0, j + off)


def _col_tile(k, tn, off=0):
    return (k, tn), lambda i, j: (0, j + off)


def _ep_plain(accs, tiles, j):
    return accs[0]


def _ep_glu(accs, tiles, j):
    return accs[0] * jax.nn.sigmoid(accs[1])


def _ep_swiglu(accs, tiles, j):
    return jax.nn.silu(accs[0]) * accs[1]


def _ep_silu_from(accs, tiles, j, *, silu_from):
    z = accs[0]
    return jnp.where(j >= silu_from, z * jax.nn.sigmoid(z), z)


def _ep_sigmoid(accs, tiles, j):
    return jax.nn.sigmoid(accs[0])


def _ep_gated_sum(accs, tiles, j):
    return accs[0] * tiles[0].astype(F32) + accs[1] * tiles[1].astype(F32)


def _ep_residual(accs, tiles, j):
    return tiles[0] + accs[0]


def _ep_ple(accs, tiles, j):
    return tiles[0] + accs[0] * jax.nn.sigmoid(accs[1])


def _conv_kernel(u_ref, halo_ref, w_ref, b_ref, g_ref, beta_ref, o_ref, xs_ref, y_ref, *, tm, seq):
    c = u_ref.shape[1]
    ext = tm + CONV_HALO - 8
    first = (pl.program_id(0) * tm) % seq == 0
    xs_ref[0, 0:CONV_HALO, :] = jnp.where(first, 0.0, halo_ref[...])
    xs_ref[0, CONV_HALO:CONV_HALO + tm, :] = u_ref[...]
    for s in range(1, 8):
        for c0 in range(0, c, V7X_LANES):
            xs_ref[s, 0:ext, c0:c0 + V7X_LANES] = xs_ref[0, s:s + ext, c0:c0 + V7X_LANES]

    lead = CONV_HALO - (CONV_WIDTH - 1)

    def conv_rows(r, carry):
        r0 = pl.multiple_of(r * CONV_ROWS, CONV_ROWS)
        for c0 in range(0, c, CONV_COLS):
            cs = slice(c0, c0 + CONV_COLS)
            accs = [b_ref[:, cs]] * (CONV_ROWS // 8)
            for w in range(CONV_WIDTH):
                off = lead + w
                base = r0 + (off // 8) * 8
                wv = w_ref[8 * w:8 * w + 8, cs]
                accs = [a + xs_ref[off % 8, pl.ds(base + 8 * g, 8), cs] * wv for g, a in enumerate(accs)]
            for g, a in enumerate(accs):
                y_ref[pl.ds(r0 + 8 * g, 8), cs] = a
        return carry

    lax.fori_loop(0, tm // CONV_ROWS, conv_rows, 0)

    def norm_rows(r, carry):
        r0 = pl.multiple_of(r * CONV_NORM_ROWS, CONV_NORM_ROWS)
        y = y_ref[pl.ds(r0, CONV_NORM_ROWS), :]
        mu = jnp.mean(y, axis=-1, keepdims=True)
        d = y - mu
        var = jnp.mean(d * d, axis=-1, keepdims=True)
        z = d * lax.rsqrt(var + EPS) * g_ref[...] + beta_ref[...]
        o_ref[pl.ds(r0, CONV_NORM_ROWS), :] = jax.nn.silu(z).astype(o_ref.dtype)
        return carry

    lax.fori_loop(0, tm // CONV_NORM_ROWS, norm_rows, 0)


def conv_branch(u, w_dw, b_dw, ln_g, ln_b, *, seq, tm=256):
    t, c = u.shape
    assert seq % tm == 0 and tm % CONV_HALO == 0 and tm % CONV_NORM_ROWS == 0
    hb = tm // CONV_HALO
    w_rep = jnp.repeat(w_dw, 8, axis=0)
    b_rep = jnp.broadcast_to(b_dw.reshape(1, c), (8, c))
    blocks = [_nbytes((tm, c), F32), _nbytes((CONV_HALO, c), F32), _nbytes(w_rep.shape, F32),
              _nbytes((tm, c), BF16)]
    scratch = 8 * _nbytes((tm + CONV_HALO, c), F32) + _nbytes((tm, c), F32)
    return pl.pallas_call(
        functools.partial(_conv_kernel, tm=tm, seq=seq),
        grid=(t // tm,),
        in_specs=[pl.BlockSpec((tm, c), lambda i: (i, 0)),
                  pl.BlockSpec((CONV_HALO, c), lambda i: (jnp.maximum(i * hb - 1, 0), 0)),
                  pl.BlockSpec(w_rep.shape, lambda i: (0, 0)),
                  pl.BlockSpec((8, c), lambda i: (0, 0)),
                  pl.BlockSpec((1, c), lambda i: (0, 0)),
                  pl.BlockSpec((1, c), lambda i: (0, 0))],
        out_specs=pl.BlockSpec((tm, c), lambda i: (i, 0)),
        out_shape=jax.ShapeDtypeStruct((t, c), BF16),
        scratch_shapes=[pltpu.VMEM((8, tm + CONV_HALO, c), F32), pltpu.VMEM((tm, c), F32)],
        compiler_params=_params(1, blocks, scratch),
        name="conv_branch",
    )(u, u, w_rep, b_rep, ln_g.reshape(1, c), ln_b.reshape(1, c))


def _gla_constants():
    cn = GLA_CHUNK
    r = np.arange(cn)[:, None]
    c = np.arange(cn)[None, :]
    parts = [c <= r, c > r]
    for lv in range(GLA_LEVELS):
        half = 1 << lv
        mid = (r // (2 * half)) * (2 * half) + half - 1
        right = (r % (2 * half)) >= half
        parts.append(np.where(right, (c > mid) & (c <= r), (c > r) & (c <= mid)))
    parts.append(np.ones((16, cn), bool))
    sums = np.concatenate(parts).astype(np.float32)
    high_bit = np.floor(np.log2(np.maximum(r ^ c, 1))).astype(np.int32)
    level_of = np.where(c < r, high_bit, np.where(c == r, GLA_LEVELS, -1)).astype(np.int32)
    return sums, level_of


def _gla_kernel(q_ref, k_ref, v_ref, og_ref, alr_ref, wup_ref, bup_ref, gn_ref,
                sums_ref, level_ref, o_ref, st_ref, la_ref, *, n_chunks, heads):
    cn = GLA_CHUNK

    @pl.when(pl.program_id(2) == 0)
    def _():
        st_ref[...] = jnp.zeros_like(st_ref)

    x = jnp.dot(alr_ref[...].astype(BF16), wup_ref[...], preferred_element_type=F32) + bup_ref[...]
    la_ref[...] = (jnp.minimum(x, 0.0) - jnp.log(1.0 + jnp.exp(-jnp.abs(x)))) * (1.0 / GLA_TAU)

    ones_dk = jnp.ones((GLA_DK, cn), BF16)
    level_of = level_ref[...]
    nt = (((1,), (1,)), ((), ()))
    tn = (((0,), (0,)), ((), ()))

    def chunk(ci, carry):
        r0 = pl.multiple_of(ci * cn, cn)
        la = la_ref[pl.ds(r0, cn), :]
        hi = la.astype(BF16)
        lo = (la - hi.astype(F32)).astype(BF16)
        sums = sums_ref[...]
        ex_all = jnp.exp(jnp.dot(sums, hi, preferred_element_type=F32)
                         + jnp.dot(sums, lo, preferred_element_type=F32))
        for h in range(heads):
            kc = slice(h * GLA_DK, (h + 1) * GLA_DK)
            vc = slice(h * GLA_DV, (h + 1) * GLA_DV)
            ex = ex_all[:, kc]
            q = q_ref[pl.ds(r0, cn), kc].astype(F32) * (GLA_DK ** -0.5)
            k = k_ref[pl.ds(r0, cn), kc].astype(F32)
            v = v_ref[pl.ds(r0, cn), vc]
            q_state = q * ex[0:cn]
            k_dec = k * ex[cn:2 * cn]
            e_last = ex[(2 + GLA_LEVELS) * cn:(2 + GLA_LEVELS) * cn + 1]

            diag = jnp.dot((q * k).astype(BF16), ones_dk, preferred_element_type=F32)
            scores = jnp.where(level_of == GLA_LEVELS, diag, 0.0)
            for lv in range(GLA_LEVELS):
                e = ex[(2 + lv) * cn:(3 + lv) * cn]
                s = lax.dot_general((q * e).astype(BF16), (k * e).astype(BF16), nt,
                                    preferred_element_type=F32)
                scores = jnp.where(level_of == lv, s, scores)

            st = st_ref[h]
            o = (jnp.dot(scores.astype(BF16), v, preferred_element_type=F32)
                 + lax.dot_general(q_state.astype(BF16), st.astype(BF16), nt, preferred_element_type=F32))
            st_ref[h] = st * e_last + lax.dot_general(v, k_dec.astype(BF16), tn, preferred_element_type=F32)

            o = o * lax.rsqrt(jnp.mean(o * o, axis=-1, keepdims=True) + EPS) * gn_ref[:, vc]
            o_ref[pl.ds(r0, cn), vc] = (o * og_ref[pl.ds(r0, cn), vc].astype(F32)).astype(o_ref.dtype)
        return carry

    lax.fori_loop(0, n_chunks, chunk, 0)


def gla_branch(z, a_lr, w_a_up, b_a, onorm_g, *, batch, seq, q_col, k_col, v_col, og_col, rows=1024, heads=8):
    t = z.shape[0]
    assert seq % rows == 0 and rows % GLA_CHUNK == 0 and GLA_HEADS % heads == 0
    nb = seq // rows
    wk, wv = heads * GLA_DK, heads * GLA_DV
    sums, level_of = _gla_constants()
    wup = jnp.zeros((V7X_LANES, GLA_HEADS * GLA_DK), F32).at[:GLA_RANK].set(w_a_up).astype(BF16)
    qb, kb, vb, ob = q_col // wk, k_col // wk, v_col // wv, og_col // wv

    def row(b, g, c):
        return b * nb + c

    blocks = [2 * _nbytes((rows, wk), BF16), 3 * _nbytes((rows, wv), BF16),
              _nbytes((rows, V7X_LANES), F32), _nbytes(sums.shape, BF16) * 2]
    scratch = heads * _nbytes((GLA_DV, GLA_DK), F32) + _nbytes((rows, wk), F32)
    return pl.pallas_call(
        functools.partial(_gla_kernel, n_chunks=rows // GLA_CHUNK, heads=heads),
        grid=(batch, GLA_HEADS // heads, nb),
        in_specs=[pl.BlockSpec((rows, wk), lambda b, g, c: (row(b, g, c), qb + g)),
                  pl.BlockSpec((rows, wk), lambda b, g, c: (row(b, g, c), kb + g)),
                  pl.BlockSpec((rows, wv), lambda b, g, c: (row(b, g, c), vb + g)),
                  pl.BlockSpec((rows, wv), lambda b, g, c: (row(b, g, c), ob + g)),
                  pl.BlockSpec((rows, V7X_LANES), lambda b, g, c: (row(b, g, c), 0)),
                  pl.BlockSpec((V7X_LANES, wk), lambda b, g, c: (0, g)),
                  pl.BlockSpec((1, wk), lambda b, g, c: (0, g)),
                  pl.BlockSpec((1, wv), lambda b, g, c: (0, g)),
                  pl.BlockSpec(sums.shape, lambda b, g, c: (0, 0)),
                  pl.BlockSpec(level_of.shape, lambda b, g, c: (0, 0))],
        out_specs=pl.BlockSpec((rows, wv), lambda b, g, c: (row(b, g, c), g)),
        out_shape=jax.ShapeDtypeStruct((t, GLA_HEADS * GLA_DV), BF16),
        scratch_shapes=[pltpu.VMEM((heads, GLA_DV, GLA_DK), F32), pltpu.VMEM((rows, wk), F32)],
        compiler_params=_params(3, blocks, scratch + (8 << 20)),
        name="gla_branch",
    )(z, z, z, z, a_lr, wup, b_a.reshape(1, -1), onorm_g.reshape(1, -1),
      jnp.asarray(sums, BF16), jnp.asarray(level_of))


MOE_TILE = 512
MOE_GATHER_ROWS = 256


def _route_kernel(x_ref, g_ref, wr_ref, br_ref, sel_ref, wts_ref):
    x = x_ref[...]
    ms = jnp.mean(x * x, axis=-1, keepdims=True)
    h = x * lax.rsqrt(ms + EPS) * g_ref[...]
    logits = jnp.dot(h, wr_ref[...], preferred_element_type=F32,
                     precision=lax.Precision.HIGHEST) + br_ref[...]
    lane = lax.broadcasted_iota(jnp.int32, logits.shape, 1)
    logits = jnp.where(lane < N_EXPERTS, logits, -jnp.inf)
    m1 = jnp.max(logits, axis=-1, keepdims=True)
    i1 = jnp.min(jnp.where(logits == m1, lane, V7X_LANES), axis=-1, keepdims=True)
    rest = jnp.where(lane == i1, -jnp.inf, logits)
    m2 = jnp.max(rest, axis=-1, keepdims=True)
    i2 = jnp.min(jnp.where(rest == m2, lane, V7X_LANES), axis=-1, keepdims=True)
    e2 = jnp.exp(m2 - m1)
    w1 = 1.0 / (1.0 + e2)
    w2 = e2 / (1.0 + e2)
    sel_ref[...] = jnp.where(lane == 0, i1, jnp.where(lane == 1, i2, 0))
    wts_ref[...] = jnp.where(lane == 0, w1, jnp.where(lane == 1, w2, 0.0))


def moe_route(x, g, router_w, router_b, tm=256):
    t, d = x.shape
    wr = jnp.zeros((d, V7X_LANES), F32).at[:, :N_EXPERTS].set(router_w)
    br = jnp.zeros((1, V7X_LANES), F32).at[0, :N_EXPERTS].set(router_b)
    blocks = [_nbytes((tm, d), F32), _nbytes((d, V7X_LANES), F32)]
    return pl.pallas_call(
        _route_kernel,
        grid=(t // tm,),
        in_specs=[pl.BlockSpec((tm, d), lambda i: (i, 0)),
                  pl.BlockSpec((1, d), lambda i: (0, 0)),
                  pl.BlockSpec((d, V7X_LANES), lambda i: (0, 0)),
                  pl.BlockSpec((1, V7X_LANES), lambda i: (0, 0))],
        out_specs=[pl.BlockSpec((tm, V7X_LANES), lambda i: (i, 0)),
                   pl.BlockSpec((tm, V7X_LANES), lambda i: (i, 0))],
        out_shape=[jax.ShapeDtypeStruct((t, V7X_LANES), jnp.int32),
                   jax.ShapeDtypeStruct((t, V7X_LANES), F32)],
        compiler_params=_params(1, blocks, 3 * _nbytes((tm, d), F32)),
        name="moe_route",
    )(x, g.reshape(1, d), wr, br)


def _moe_plan(sel, wts):
    t = sel.shape[0]
    na = 2 * t
    e = sel[:, :2].reshape(na)
    w = wts[:, :2].reshape(na)
    ids = jnp.arange(N_EXPERTS, dtype=jnp.int32)
    order = jnp.argsort(e, stable=True).astype(jnp.int32)
    counts = jnp.sum((e[:, None] == ids[None, :]).astype(jnp.int32), axis=0)
    padded = (counts + MOE_TILE - 1) // MOE_TILE * MOE_TILE
    pend = jnp.cumsum(padded)
    pstart = pend - padded
    cstart = jnp.cumsum(counts) - counts
    n_rows = na + N_EXPERTS * MOE_TILE
    r = jnp.arange(n_rows, dtype=jnp.int32)
    e_r = jnp.minimum(jnp.sum((r[:, None] >= pend[None, :]).astype(jnp.int32), axis=1), N_EXPERTS - 1)
    local = r - pstart[e_r]
    valid = local < counts[e_r]
    a_r = order[jnp.clip(cstart[e_r] + local, 0, na - 1)]
    src = jnp.where(valid, a_r // 2, 0)
    w_row = jnp.where(valid, w[a_r], 0.0)
    inv = jnp.zeros((na,), jnp.int32).at[order].set(jnp.arange(na, dtype=jnp.int32))
    pos = (inv + (pstart - cstart)[e]).reshape(t, 2)
    tile_expert = e_r[::MOE_TILE]
    tile_used = (r[::MOE_TILE] < pend[N_EXPERTS - 1]).astype(jnp.int32)
    return src, w_row, pos, tile_expert, tile_used


def _start_row_gather(idx_ref, src_hbm, dst_ref, sem, n):
    def body(r, carry):
        pltpu.make_async_copy(src_hbm.at[pl.ds(idx_ref[0, 0, r], 1)], dst_ref.at[pl.ds(r, 1)], sem).start()
        return carry

    lax.fori_loop(0, n, body, 0)


def _wait_row_gather(src_hbm, dst_ref, sem, n):
    pltpu.make_async_copy(src_hbm.at[pl.ds(0, n)], dst_ref.at[pl.ds(0, n)], sem).wait()


def _pipelined_row_gather(idx_ref, idx_next_ref, src_hbm, buf_ref, sem_ref, n):
    i = pl.program_id(0)
    slot = i % 2

    @pl.when(i == 0)
    def _():
        _start_row_gather(idx_ref, src_hbm, buf_ref.at[0], sem_ref.at[0], n)

    @pl.when(i + 1 < pl.num_programs(0))
    def _():
        _start_row_gather(idx_next_ref, src_hbm, buf_ref.at[1 - slot], sem_ref.at[1 - slot], n)

    _wait_row_gather(src_hbm, buf_ref.at[slot], sem_ref.at[slot], n)
    return slot


def _gather_norm_kernel(idx_ref, idx_next_ref, x_hbm, g_ref, o_ref, buf_ref, sem_ref):
    slot = _pipelined_row_gather(idx_ref, idx_next_ref, x_hbm, buf_ref, sem_ref, o_ref.shape[0])
    x = buf_ref[slot]
    ms = jnp.mean(x * x, axis=-1, keepdims=True)
    o_ref[...] = (x * lax.rsqrt(ms + EPS) * g_ref[...]).astype(o_ref.dtype)


def _row_index_specs(idx, rows):
    n = idx.shape[0] // rows
    idx3 = idx.reshape(n, 1, rows)
    cur = pl.BlockSpec((1, 1, rows), lambda i: (i, 0, 0), memory_space=pltpu.SMEM)
    nxt = pl.BlockSpec((1, 1, rows), lambda i: (jnp.minimum(i + 1, n - 1), 0, 0), memory_space=pltpu.SMEM)
    return idx3, cur, nxt


def moe_gather_norm(x, g, src):
    t, d = x.shape
    rows = 2 * MOE_GATHER_ROWS
    n_rows = src.shape[0]
    idx3, cur, nxt = _row_index_specs(src, rows)
    blocks = [_nbytes((rows, d), BF16)]
    scratch = 2 * _nbytes((rows, d), F32)
    return pl.pallas_call(
        _gather_norm_kernel,
        grid=(n_rows // rows,),
        in_specs=[cur, nxt, pl.BlockSpec(memory_space=pl.ANY), pl.BlockSpec((1, d), lambda i: (0, 0))],
        out_specs=pl.BlockSpec((rows, d), lambda i: (i, 0)),
        out_shape=jax.ShapeDtypeStruct((n_rows, d), BF16),
        scratch_shapes=[pltpu.VMEM((2, rows, d), F32), pltpu.SemaphoreType.DMA((2,))],
        compiler_params=_params(1, blocks, scratch + 3 * _nbytes((rows, d), F32)),
        name="moe_gather_norm",
    )(idx3, idx3, x, g.reshape(1, d))


def _expert_kernel(te_ref, tu_ref, h_ref, wrow_ref, wg_ref, wu_ref, wd_ref, o_ref, acc_ref):
    j = pl.program_id(1)

    @pl.when(j == 0)
    def _():
        acc_ref[...] = jnp.zeros_like(acc_ref)

    @pl.when(tu_ref[pl.program_id(0)] != 0)
    def _():
        h = h_ref[...]
        a = (jax.nn.silu(jnp.dot(h, wg_ref[...], preferred_element_type=F32))
             * jnp.dot(h, wu_ref[...], preferred_element_type=F32) * wrow_ref[...])
        acc_ref[...] += jnp.dot(a.astype(BF16), wd_ref[...], preferred_element_type=F32)

    @pl.when(j == pl.num_programs(1) - 1)
    def _():
        o_ref[...] = acc_ref[...]


def moe_experts(h_sorted, w_row, tile_expert, tile_used, w_gate, w_up, w_down, th=256):
    n_rows, d = h_sorted.shape
    fe = w_gate.shape[2]
    tm = MOE_TILE
    nj = fe // th

    def hidden(i, j, te, tu):
        return jnp.where(tu[i] != 0, j, nj - 1)

    blocks = [_nbytes((tm, d), BF16), _nbytes((tm, V7X_LANES), F32), 3 * _nbytes((d, th), BF16),
              _nbytes((tm, d), F32)]
    return pl.pallas_call(
        _expert_kernel,
        grid_spec=pltpu.PrefetchScalarGridSpec(
            num_scalar_prefetch=2,
            grid=(n_rows // tm, nj),
            in_specs=[pl.BlockSpec((tm, d), lambda i, j, te, tu: (i, 0)),
                      pl.BlockSpec((tm, 1), lambda i, j, te, tu: (i, 0)),
                      pl.BlockSpec((None, d, th), lambda i, j, te, tu: (te[i], 0, hidden(i, j, te, tu))),
                      pl.BlockSpec((None, d, th), lambda i, j, te, tu: (te[i], 0, hidden(i, j, te, tu))),
                      pl.BlockSpec((None, th, d), lambda i, j, te, tu: (te[i], hidden(i, j, te, tu), 0))],
            out_specs=pl.BlockSpec((tm, d), lambda i, j, te, tu: (i, 0)),
            scratch_shapes=[pltpu.VMEM((tm, d), F32)]),
        out_shape=jax.ShapeDtypeStruct((n_rows, d), F32),
        compiler_params=_params(2, blocks, 2 * _nbytes((tm, d), F32)),
        name="moe_experts",
    )(tile_expert, tile_used, h_sorted, w_row.reshape(n_rows, 1), w_gate, w_up, w_down)


def _combine_kernel(idx_ref, idx_next_ref, y_hbm, x_ref, o_ref, buf_ref, sem_ref):
    tc = x_ref.shape[0]
    slot = _pipelined_row_gather(idx_ref, idx_next_ref, y_hbm, buf_ref, sem_ref, 2 * tc)
    o_ref[...] = x_ref[...] + buf_ref[slot, 0:tc, :] + buf_ref[slot, tc:2 * tc, :]


def moe_combine(x, y_sorted, pos):
    t, d = x.shape
    tc = MOE_GATHER_ROWS
    n = t // tc
    idx = pos.reshape(n, tc, 2).transpose(0, 2, 1).reshape(n * 2 * tc)
    idx3, cur, nxt = _row_index_specs(idx, 2 * tc)
    blocks = [2 * _nbytes((tc, d), F32)]
    scratch = 2 * _nbytes((2 * tc, d), F32)
    return pl.pallas_call(
        _combine_kernel,
        grid=(n,),
        in_specs=[cur, nxt, pl.BlockSpec(memory_space=pl.ANY), pl.BlockSpec((tc, d), lambda i: (i, 0))],
        out_specs=pl.BlockSpec((tc, d), lambda i: (i, 0)),
        out_shape=jax.ShapeDtypeStruct((t, d), F32),
        scratch_shapes=[pltpu.VMEM((2, 2 * tc, d), F32), pltpu.SemaphoreType.DMA((2,))],
        compiler_params=_params(1, blocks, scratch),
        name="moe_combine",
    )(idx3, idx3, y_sorted, x)


def _layer(x, p_bf, w, li, *, batch, seq):
    t, d = x.shape
    cdim = w["conv_dw"].shape[2]
    kdim, vdim = GLA_HEADS * GLA_DK, GLA_HEADS * GLA_DV
    tm, tn, tn2 = 1024, 512, 256
    tnw = 1024
    fj = li // 2

    h = rmsnorm(x, w["norm_mix"][li], BF16)
    w_in = w["w_in"]
    o_q = 2 * cdim
    o_alr = o_q + 2 * kdim + 2 * vdim
    o_gate = o_alr + GLA_RANK
    c_og = 2 * kdim + vdim
    w_gate = w_in[li, :, o_gate:]
    w_alr = jnp.zeros((d, V7X_LANES), BF16).at[:, :GLA_RANK].set(w_in[li, :, o_alr:o_gate])
    u = fused_matmul([h], [(w_in, *_layer_col_tile(li, d, tn)), (w_in, *_layer_col_tile(li, d, tn, cdim // tn))],
                     [], _ep_glu, n_out=cdim, tm=tm, tn=tn, out_dtype=F32, name="in_glu", pair_a=(0, 0))
    z = fused_matmul([h], [(w_in, *_layer_col_tile(li, d, tnw, o_q // tnw))], [],
                     functools.partial(_ep_silu_from, silu_from=c_og // tnw),
                     n_out=o_alr - o_q, tm=tm, tn=tnw, out_dtype=BF16, name="in_z")
    gates = fused_matmul([h], [(w_gate, *_col_tile(d, tnw))], [], _ep_sigmoid,
                         n_out=2 * d, tm=tm, tn=tnw, out_dtype=BF16, name="in_gate")
    a_lr = fused_matmul([h], [(w_alr, *_col_tile(d, V7X_LANES))], [], _ep_plain,
                        n_out=V7X_LANES, tm=tm, tn=V7X_LANES, out_dtype=F32, name="in_alr")

    y_conv = conv_branch(u, w["conv_dw"][li], w["conv_dw_b"][li], w["conv_ln_g"][li], w["conv_ln_b"][li], seq=seq)
    y_gla = gla_branch(z, a_lr, w["gla_a_up"][li], w["gla_a_b"][li], w["gla_onorm"][li], batch=batch, seq=seq,
                       q_col=0, k_col=kdim, v_col=2 * kdim, og_col=c_og)

    def tile(off=0):
        return (tm, tn), lambda i, j: (i, j + off // tn)

    mix = fused_matmul([y_conv, y_gla],
                       [(w["conv_w_out"], *_layer_col_tile(li, cdim, tn)),
                        (w["gla_w_out"], *_layer_col_tile(li, vdim, tn))],
                       [(gates, *tile()), (gates, *tile(d))],
                       _ep_gated_sum, n_out=d, tm=tm, tn=tn, out_dtype=BF16, name="branch_out", w_resident=True)
    x = fused_matmul([mix], [(w["w_o"], *_layer_col_tile(li, d, tn))], [(x, *tile())],
                     _ep_residual, n_out=d, tm=tm, tn=tn, out_dtype=F32, name="w_o", w_resident=True)

    if li % 2 == 0:
        h = rmsnorm(x, w["norm_ffn"][li], BF16)
        f = w["ffn_w_gate"].shape[2]
        act = fused_matmul([h], [(w["ffn_w_gate"], *_layer_col_tile(fj, d, tn2)),
                                 (w["ffn_w_up"], *_layer_col_tile(fj, d, tn2))], [],
                           _ep_swiglu, n_out=f, tm=tm, tn=tn2, out_dtype=BF16, name="ffn_up", pair_a=(0, 0),
                           w_resident=True)
        x = fused_matmul([act], [(w["ffn_w_down"][fj].astype(BF16), *_col_tile(f, tn))], [(x, *tile())],
                         _ep_residual, n_out=d, tm=tm, tn=tn, out_dtype=F32, name="ffn_down")
    else:
        sel, wts = moe_route(x, w["norm_ffn"][li], w["router_w"][fj], w["router_b"][fj])
        src, w_row, pos, tile_expert, tile_used = _moe_plan(sel, wts)
        h_sorted = moe_gather_norm(x, w["norm_ffn"][li], src)
        y_sorted = moe_experts(h_sorted, w_row, tile_expert, tile_used, w["moe_w_gate"][fj].astype(BF16),
                               w["moe_w_up"][fj].astype(BF16), w["moe_w_down"][fj].astype(BF16))
        x = moe_combine(x, y_sorted, pos)

    h = rmsnorm(x, w["ple_norm"][li], BF16)
    pd = w["ple_gate_down"].shape[2]
    g1 = fused_matmul([h], [(w["ple_gate_down"][li].astype(BF16), *_col_tile(d, pd))], [], _ep_plain,
                      n_out=pd, tm=tm, tn=pd, out_dtype=BF16, name="ple_down")
    x = fused_matmul([p_bf, g1], [(w["ple_proj"][li].astype(BF16), *_col_tile(pd, tn)),
                                  (w["ple_gate_up"][li].astype(BF16), *_col_tile(pd, tn))],
                     [(x, *tile())], _ep_ple, n_out=d, tm=tm, tn=tn, out_dtype=F32, name="ple_up")
    return x


def kernel(x, p, norm_mix, w_in, conv_dw, conv_dw_b, conv_ln_g, conv_ln_b, conv_w_out, gla_a_up, gla_a_b, gla_onorm, gla_w_out, w_o, norm_ffn, ffn_w_gate, ffn_w_up, ffn_w_down, router_w, router_b, moe_w_gate, moe_w_up, moe_w_down, ple_norm, ple_gate_down, ple_gate_up, ple_proj, final_norm):
    batch, seq, d = x.shape
    w = dict(norm_mix=norm_mix, w_in=w_in.astype(BF16), conv_dw=conv_dw, conv_dw_b=conv_dw_b, conv_ln_g=conv_ln_g,
             conv_ln_b=conv_ln_b, conv_w_out=conv_w_out, gla_a_up=gla_a_up, gla_a_b=gla_a_b, gla_onorm=gla_onorm,
             gla_w_out=gla_w_out, w_o=w_o, norm_ffn=norm_ffn, ffn_w_gate=ffn_w_gate, ffn_w_up=ffn_w_up,
             ffn_w_down=ffn_w_down, router_w=router_w, router_b=router_b, moe_w_gate=moe_w_gate,
             moe_w_up=moe_w_up, moe_w_down=moe_w_down, ple_norm=ple_norm, ple_gate_down=ple_gate_down,
             ple_gate_up=ple_gate_up, ple_proj=ple_proj)
    xt = x.reshape(batch * seq, d)
    for li in range(w_in.shape[0]):
        p_bf = p[li].reshape(batch * seq, -1).astype(BF16)
        xt = _layer(xt, p_bf, w, li, batch=batch, seq=seq)
    return rmsnorm(xt, final_norm, F32).reshape(batch, seq, d)
```

```python
import functools

import numpy as np
import jax
import jax.numpy as jnp
from jax import lax
from jax.experimental import pallas as pl
from jax.experimental.pallas import tpu as pltpu

F32 = jnp.float32
BF16 = jnp.bfloat16

EPS = 1e-6
CONV_WIDTH = 31
CONV_HALO = 32
CONV_NORM_ROWS = 64
CONV_ROWS = 32
CONV_COLS = 512
GLA_HEADS = 8
GLA_DK = 128
GLA_DV = 256
GLA_RANK = 16
GLA_TAU = 16.0
GLA_CHUNK = 64
GLA_LEVELS = 6
N_EXPERTS = 8

V7X_LANES = 128
V7X_SCOPED_VMEM_CAP = 60000 * 1024


def _vmem_limit(block_bytes, temp_bytes=0):
    need = 2 * sum(block_bytes) + temp_bytes
    return int(min(V7X_SCOPED_VMEM_CAP, need + need // 4 + (4 << 20)))


def _nbytes(shape, dtype):
    return int(np.prod(shape)) * jnp.dtype(dtype).itemsize


def _params(n_grid, block_bytes, temp_bytes=0):
    return pltpu.CompilerParams(
        dimension_semantics=("arbitrary",) * n_grid,
        vmem_limit_bytes=_vmem_limit(block_bytes, temp_bytes))


NORM_ROWS = 64


def _rms(x, g):
    return x * lax.rsqrt(jnp.mean(x * x, axis=-1, keepdims=True) + EPS) * g


def _rms_kernel(x_ref, g_ref, o_ref):
    o_ref[...] = _rms(x_ref[...], g_ref[...]).astype(o_ref.dtype)


def rmsnorm(x, g, out_dtype, tm=256):
    t, d = x.shape
    blocks = [_nbytes((tm, d), F32), _nbytes((tm, d), out_dtype)]
    return pl.pallas_call(
        _rms_kernel,
        grid=(t // tm,),
        in_specs=[pl.BlockSpec((tm, d), lambda i: (i, 0)),
                  pl.BlockSpec((1, d), lambda i: (0, 0))],
        out_specs=pl.BlockSpec((tm, d), lambda i: (i, 0)),
        out_shape=jax.ShapeDtypeStruct((t, d), out_dtype),
        compiler_params=_params(1, blocks, 3 * _nbytes((tm, d), F32)),
        name="rmsnorm",
    )(x, g.reshape(1, d))


def _mm_kernel(*refs, pair_a, n_a, n_t, epilogue, col_axis, cast_w):
    n_w = len(pair_a)
    a_refs = refs[:n_a]
    w_refs = refs[n_a:n_a + n_w]
    t_refs = refs[n_a + n_w:n_a + n_w + n_t]
    o_ref = refs[n_a + n_w + n_t]
    if cast_w:
        wb_refs = refs[n_a + n_w + n_t + 1:]

        @pl.when(pl.program_id(1) == 0)
        def _():
            for w_ref, wb_ref in zip(w_refs, wb_refs):
                wb_ref[...] = w_ref[...].astype(BF16)

        w_refs = wb_refs
    accs = [jnp.dot(a_refs[ai][...], w_ref[...], preferred_element_type=F32)
            for ai, w_ref in zip(pair_a, w_refs)]
    tiles = [t_ref[...] for t_ref in t_refs]
    o_ref[...] = epilogue(accs, tiles, pl.program_id(col_axis)).astype(o_ref.dtype)


def fused_matmul(a_list, w_list, t_list, epilogue, *, n_out, tm, tn, out_dtype, name, pair_a=None,
                 w_resident=False):
    m = a_list[0].shape[0]
    pair_a = tuple(range(len(w_list))) if pair_a is None else tuple(pair_a)
    cast_w = any(w.dtype != BF16 for w, _, _ in w_list)
    assert w_resident or not cast_w

    def order(im):
        return (lambda j, i: im(i, j)) if w_resident else im

    in_specs = [pl.BlockSpec((tm, a.shape[1]), order(lambda i, j: (i, 0))) for a in a_list]
    in_specs += [pl.BlockSpec(bs, order(im)) for _, bs, im in w_list]
    in_specs += [pl.BlockSpec(bs, order(im)) for _, bs, im in t_list]
    w_shapes = [tuple(s for s in bs if s is not None) for _, bs, _ in w_list]
    blocks = [_nbytes((tm, a.shape[1]), a.dtype) for a in a_list]
    blocks += [_nbytes(ws, w.dtype) for (w, _, _), ws in zip(w_list, w_shapes)]
    blocks += [_nbytes(bs, t.dtype) for t, bs, _ in t_list]
    blocks += [_nbytes((tm, tn), out_dtype)]
    temps = (len(w_list) + 1) * _nbytes((tm, tn), F32)
    scratch = [pltpu.VMEM(ws, BF16) for ws in w_shapes] if cast_w else []
    temps += sum(_nbytes(ws, BF16) for ws in w_shapes) if cast_w else 0
    grid = (n_out // tn, m // tm) if w_resident else (m // tm, n_out // tn)
    return pl.pallas_call(
        functools.partial(_mm_kernel, pair_a=pair_a, n_a=len(a_list), n_t=len(t_list), epilogue=epilogue,
                          col_axis=0 if w_resident else 1, cast_w=cast_w),
        grid=grid,
        in_specs=in_specs,
        out_specs=pl.BlockSpec((tm, tn), order(lambda i, j: (i, j))),
        out_shape=jax.ShapeDtypeStruct((m, n_out), out_dtype),
        scratch_shapes=scratch,
        compiler_params=_params(2, blocks, temps),
        name=name,
    )(*a_list, *[w for w, _, _ in w_list], *[t for t, _, _ in t_list])


def _layer_col_tile(layer, k, tn, off=0):
    return (None, k, tn), lambda i, j: (layer, 0, j + off)


def _col_tile(k, tn, off=0):
    return (k, tn), lambda i, j: (0, j + off)


def _ep_plain(accs, tiles, j):
    return accs[0]


def _ep_glu(accs, tiles, j):
    return accs[0] * jax.nn.sigmoid(accs[1])


def _ep_swiglu(accs, tiles, j):
    return jax.nn.silu(accs[0]) * accs[1]


def _ep_silu_from(accs, tiles, j, *, silu_from):
    z = accs[0]
    return jnp.where(j >= silu_from, z * jax.nn.sigmoid(z), z)


def _ep_sigmoid(accs, tiles, j):
    return jax.nn.sigmoid(accs[0])


def _ep_gated_sum(accs, tiles, j):
    return accs[0] * tiles[0].astype(F32) + accs[1] * tiles[1].astype(F32)


def _ep_residual(accs, tiles, j):
    return tiles[0] + accs[0]


def _conv_kernel(u_ref, halo_ref, w_ref, b_ref, g_ref, beta_ref, o_ref, xs_ref, y_ref, *, tm, seq):
    c = u_ref.shape[1]
    ext = tm + CONV_HALO - 8
    first = (pl.program_id(0) * tm) % seq == 0
    xs_ref[0, 0:CONV_HALO, :] = jnp.where(first, 0.0, halo_ref[...])
    xs_ref[0, CONV_HALO:CONV_HALO + tm, :] = u_ref[...]
    for s in range(1, 8):
        for c0 in range(0, c, V7X_LANES):
            xs_ref[s, 0:ext, c0:c0 + V7X_LANES] = xs_ref[0, s:s + ext, c0:c0 + V7X_LANES]

    lead = CONV_HALO - (CONV_WIDTH - 1)

    def conv_rows(r, carry):
        r0 = pl.multiple_of(r * CONV_ROWS, CONV_ROWS)
        for c0 in range(0, c, CONV_COLS):
            cs = slice(c0, c0 + CONV_COLS)
            accs = [b_ref[:, cs]] * (CONV_ROWS // 8)
            for w in range(CONV_WIDTH):
                off = lead + w
                base = r0 + (off // 8) * 8
                wv = w_ref[8 * w:8 * w + 8, cs]
                accs = [a + xs_ref[off % 8, pl.ds(base + 8 * g, 8), cs] * wv for g, a in enumerate(accs)]
            for g, a in enumerate(accs):
                y_ref[pl.ds(r0 + 8 * g, 8), cs] = a
        return carry

    lax.fori_loop(0, tm // CONV_ROWS, conv_rows, 0)

    def norm_rows(r, carry):
        r0 = pl.multiple_of(r * CONV_NORM_ROWS, CONV_NORM_ROWS)
        y = y_ref[pl.ds(r0, CONV_NORM_ROWS), :]
        mu = jnp.mean(y, axis=-1, keepdims=True)
        d = y - mu
        var = jnp.mean(d * d, axis=-1, keepdims=True)
        z = d * lax.rsqrt(var + EPS) * g_ref[...] + beta_ref[...]
        o_ref[pl.ds(r0, CONV_NORM_ROWS), :] = jax.nn.silu(z).astype(o_ref.dtype)
        return carry

    lax.fori_loop(0, tm // CONV_NORM_ROWS, norm_rows, 0)


def conv_branch(u, w_dw, b_dw, ln_g, ln_b, *, seq, tm=256):
    t, c = u.shape
    assert seq % tm == 0 and tm % CONV_HALO == 0 and tm % CONV_NORM_ROWS == 0
    hb = tm // CONV_HALO
    w_rep = jnp.repeat(w_dw, 8, axis=0)
    b_rep = jnp.broadcast_to(b_dw.reshape(1, c), (8, c))
    blocks = [_nbytes((tm, c), F32), _nbytes((CONV_HALO, c), F32), _nbytes(w_rep.shape, F32),
              _nbytes((tm, c), BF16)]
    scratch = 8 * _nbytes((tm + CONV_HALO, c), F32) + _nbytes((tm, c), F32)
    return pl.pallas_call(
        functools.partial(_conv_kernel, tm=tm, seq=seq),
        grid=(t // tm,),
        in_specs=[pl.BlockSpec((tm, c), lambda i: (i, 0)),
                  pl.BlockSpec((CONV_HALO, c), lambda i: (jnp.maximum(i * hb - 1, 0), 0)),
                  pl.BlockSpec(w_rep.shape, lambda i: (0, 0)),
                  pl.BlockSpec((8, c), lambda i: (0, 0)),
                  pl.BlockSpec((1, c), lambda i: (0, 0)),
                  pl.BlockSpec((1, c), lambda i: (0, 0))],
        out_specs=pl.BlockSpec((tm, c), lambda i: (i, 0)),
        out_shape=jax.ShapeDtypeStruct((t, c), BF16),
        scratch_shapes=[pltpu.VMEM((8, tm + CONV_HALO, c), F32), pltpu.VMEM((tm, c), F32)],
        compiler_params=_params(1, blocks, scratch),
        name="conv_branch",
    )(u, u, w_rep, b_rep, ln_g.reshape(1, c), ln_b.reshape(1, c))


def _gla_constants():
    cn = GLA_CHUNK
    r = np.arange(cn)[:, None]
    c = np.arange(cn)[None, :]
    parts = [c <= r, c > r]
    for lv in range(GLA_LEVELS):
        half = 1 << lv
        mid = (r // (2 * half)) * (2 * half) + half - 1
        right = (r % (2 * half)) >= half
        parts.append(np.where(right, (c > mid) & (c <= r), (c > r) & (c <= mid)))
    parts.append(np.ones((16, cn), bool))
    sums = np.concatenate(parts).astype(np.float32)
    high_bit = np.floor(np.log2(np.maximum(r ^ c, 1))).astype(np.int32)
    level_of = np.where(c < r, high_bit, np.where(c == r, GLA_LEVELS, -1)).astype(np.int32)
    return sums, level_of


def _gla_kernel(q_ref, k_ref, v_ref, og_ref, alr_ref, wup_ref, bup_ref, gn_ref,
                sums_ref, level_ref, o_ref, st_ref, la_ref, *, n_chunks, heads):
    cn = GLA_CHUNK

    @pl.when(pl.program_id(2) == 0)
    def _():
        st_ref[...] = jnp.zeros_like(st_ref)

    x = jnp.dot(alr_ref[...].astype(BF16), wup_ref[...], preferred_element_type=F32) + bup_ref[...]
    la_ref[...] = (jnp.minimum(x, 0.0) - jnp.log(1.0 + jnp.exp(-jnp.abs(x)))) * (1.0 / GLA_TAU)

    ones_dk = jnp.ones((GLA_DK, cn), BF16)
    level_of = level_ref[...]
    nt = (((1,), (1,)), ((), ()))
    tn = (((0,), (0,)), ((), ()))

    def chunk(ci, carry):
        r0 = pl.multiple_of(ci * cn, cn)
        la = la_ref[pl.ds(r0, cn), :]
        hi = la.astype(BF16)
        lo = (la - hi.astype(F32)).astype(BF16)
        sums = sums_ref[...]
        ex_all = jnp.exp(jnp.dot(sums, hi, preferred_element_type=F32)
                         + jnp.dot(sums, lo, preferred_element_type=F32))
        for h in range(heads):
            kc = slice(h * GLA_DK, (h + 1) * GLA_DK)
            vc = slice(h * GLA_DV, (h + 1) * GLA_DV)
            ex = ex_all[:, kc]
            q = q_ref[pl.ds(r0, cn), kc].astype(F32) * (GLA_DK ** -0.5)
            k = k_ref[pl.ds(r0, cn), kc].astype(F32)
            v = v_ref[pl.ds(r0, cn), vc]
            q_state = q * ex[0:cn]
            k_dec = k * ex[cn:2 * cn]
            e_last = ex[(2 + GLA_LEVELS) * cn:(2 + GLA_LEVELS) * cn + 1]

            diag = jnp.dot((q * k).astype(BF16), ones_dk, preferred_element_type=F32)
            scores = jnp.where(level_of == GLA_LEVELS, diag, 0.0)
            for lv in range(GLA_LEVELS):
                e = ex[(2 + lv) * cn:(3 + lv) * cn]
                s = lax.dot_general((q * e).astype(BF16), (k * e).astype(BF16), nt,
                                    preferred_element_type=F32)
                scores = jnp.where(level_of == lv, s, scores)

            st = st_ref[h]
            o = (jnp.dot(scores.astype(BF16), v, preferred_element_type=F32)
                 + lax.dot_general(q_state.astype(BF16), st.astype(BF16), nt, preferred_element_type=F32))
            st_ref[h] = st * e_last + lax.dot_general(v, k_dec.astype(BF16), tn, preferred_element_type=F32)

            o = o * lax.rsqrt(jnp.mean(o * o, axis=-1, keepdims=True) + EPS) * gn_ref[:, vc]
            o_ref[pl.ds(r0, cn), vc] = (o * og_ref[pl.ds(r0, cn), vc].astype(F32)).astype(o_ref.dtype)
        return carry

    lax.fori_loop(0, n_chunks, chunk, 0)


def gla_branch(z, a_lr, w_a_up, b_a, onorm_g, *, batch, seq, q_col, k_col, v_col, og_col, rows=1024, heads=8):
    t = z.shape[0]
    assert seq % rows == 0 and rows % GLA_CHUNK == 0 and GLA_HEADS % heads == 0
    nb = seq // rows
    wk, wv = heads * GLA_DK, heads * GLA_DV
    sums, level_of = _gla_constants()
    wup = jnp.zeros((V7X_LANES, GLA_HEADS * GLA_DK), F32).at[:GLA_RANK].set(w_a_up).astype(BF16)
    qb, kb, vb, ob = q_col // wk, k_col // wk, v_col // wv, og_col // wv

    def row(b, g, c):
        return b * nb + c

    blocks = [2 * _nbytes((rows, wk), BF16), 3 * _nbytes((rows, wv), BF16),
              _nbytes((rows, V7X_LANES), F32), _nbytes(sums.shape, BF16) * 2]
    scratch = heads * _nbytes((GLA_DV, GLA_DK), F32) + _nbytes((rows, wk), F32)
    return pl.pallas_call(
        functools.partial(_gla_kernel, n_chunks=rows // GLA_CHUNK, heads=heads),
        grid=(batch, GLA_HEADS // heads, nb),
        in_specs=[pl.BlockSpec((rows, wk), lambda b, g, c: (row(b, g, c), qb + g)),
                  pl.BlockSpec((rows, wk), lambda b, g, c: (row(b, g, c), kb + g)),
                  pl.BlockSpec((rows, wv), lambda b, g, c: (row(b, g, c), vb + g)),
                  pl.BlockSpec((rows, wv), lambda b, g, c: (row(b, g, c), ob + g)),
                  pl.BlockSpec((rows, V7X_LANES), lambda b, g, c: (row(b, g, c), 0)),
                  pl.BlockSpec((V7X_LANES, wk), lambda b, g, c: (0, g)),
                  pl.BlockSpec((1, wk), lambda b, g, c: (0, g)),
                  pl.BlockSpec((1, wv), lambda b, g, c: (0, g)),
                  pl.BlockSpec(sums.shape, lambda b, g, c: (0, 0)),
                  pl.BlockSpec(level_of.shape, lambda b, g, c: (0, 0))],
        out_specs=pl.BlockSpec((rows, wv), lambda b, g, c: (row(b, g, c), g)),
        out_shape=jax.ShapeDtypeStruct((t, GLA_HEADS * GLA_DV), BF16),
        scratch_shapes=[pltpu.VMEM((heads, GLA_DV, GLA_DK), F32), pltpu.VMEM((rows, wk), F32)],
        compiler_params=_params(3, blocks, scratch + (8 << 20)),
        name="gla_branch",
    )(z, z, z, z, a_lr, wup, b_a.reshape(1, -1), onorm_g.reshape(1, -1),
      jnp.asarray(sums, BF16), jnp.asarray(level_of))


MOE_TILE = 512
MOE_GATHER_ROWS = 256


def _route_kernel(x_ref, g_ref, wr_ref, br_ref, sel_ref, wts_ref):
    x = x_ref[...]
    ms = jnp.mean(x * x, axis=-1, keepdims=True)
    h = x * lax.rsqrt(ms + EPS) * g_ref[...]
    logits = jnp.dot(h, wr_ref[...], preferred_element_type=F32,
                     precision=lax.Precision.HIGHEST) + br_ref[...]
    lane = lax.broadcasted_iota(jnp.int32, logits.shape, 1)
    logits = jnp.where(lane < N_EXPERTS, logits, -jnp.inf)
    m1 = jnp.max(logits, axis=-1, keepdims=True)
    i1 = jnp.min(jnp.where(logits == m1, lane, V7X_LANES), axis=-1, keepdims=True)
    rest = jnp.where(lane == i1, -jnp.inf, logits)
    m2 = jnp.max(rest, axis=-1, keepdims=True)
    i2 = jnp.min(jnp.where(rest == m2, lane, V7X_LANES), axis=-1, keepdims=True)
    e2 = jnp.exp(m2 - m1)
    w1 = 1.0 / (1.0 + e2)
    w2 = e2 / (1.0 + e2)
    sel_ref[...] = jnp.where(lane == 0, i1, jnp.where(lane == 1, i2, 0))
    wts_ref[...] = jnp.where(lane == 0, w1, jnp.where(lane == 1, w2, 0.0))


def moe_route(x, g, router_w, router_b, tm=256):
    t, d = x.shape
    wr = jnp.zeros((d, V7X_LANES), F32).at[:, :N_EXPERTS].set(router_w)
    br = jnp.zeros((1, V7X_LANES), F32).at[0, :N_EXPERTS].set(router_b)
    blocks = [_nbytes((tm, d), F32), _nbytes((d, V7X_LANES), F32)]
    return pl.pallas_call(
        _route_kernel,
        grid=(t // tm,),
        in_specs=[pl.BlockSpec((tm, d), lambda i: (i, 0)),
                  pl.BlockSpec((1, d), lambda i: (0, 0)),
                  pl.BlockSpec((d, V7X_LANES), lambda i: (0, 0)),
                  pl.BlockSpec((1, V7X_LANES), lambda i: (0, 0))],
        out_specs=[pl.BlockSpec((tm, V7X_LANES), lambda i: (i, 0)),
                   pl.BlockSpec((tm, V7X_LANES), lambda i: (i, 0))],
        out_shape=[jax.ShapeDtypeStruct((t, V7X_LANES), jnp.int32),
                   jax.ShapeDtypeStruct((t, V7X_LANES), F32)],
        compiler_params=_params(1, blocks, 3 * _nbytes((tm, d), F32)),
        name="moe_route",
    )(x, g.reshape(1, d), wr, br)


def _moe_plan(sel, wts):
    t = sel.shape[0]
    na = 2 * t
    e = sel[:, :2].reshape(na)
    w = wts[:, :2].reshape(na)
    ids = jnp.arange(N_EXPERTS, dtype=jnp.int32)
    order = jnp.argsort(e, stable=True).astype(jnp.int32)
    counts = jnp.sum((e[:, None] == ids[None, :]).astype(jnp.int32), axis=0)
    padded = (counts + MOE_TILE - 1) // MOE_TILE * MOE_TILE
    pend = jnp.cumsum(padded)
    pstart = pend - padded
    cstart = jnp.cumsum(counts) - counts
    n_rows = na + N_EXPERTS * MOE_TILE
    r = jnp.arange(n_rows, dtype=jnp.int32)
    e_r = jnp.minimum(jnp.sum((r[:, None] >= pend[None, :]).astype(jnp.int32), axis=1), N_EXPERTS - 1)
    local = r - pstart[e_r]
    valid = local < counts[e_r]
    a_r = order[jnp.clip(cstart[e_r] + local, 0, na - 1)]
    src = jnp.where(valid, a_r // 2, 0)
    w_row = jnp.where(valid, w[a_r], 0.0)
    inv = jnp.zeros((na,), jnp.int32).at[order].set(jnp.arange(na, dtype=jnp.int32))
    pos = (inv + (pstart - cstart)[e]).reshape(t, 2)
    tile_expert = e_r[::MOE_TILE]
    tile_used = (r[::MOE_TILE] < pend[N_EXPERTS - 1]).astype(jnp.int32)
    return src, w_row, pos, tile_expert, tile_used


def _start_row_gather(idx_ref, src_hbm, dst_ref, sem, n):
    def body(r, carry):
        pltpu.make_async_copy(src_hbm.at[pl.ds(idx_ref[0, 0, r], 1)], dst_ref.at[pl.ds(r, 1)], sem).start()
        return carry

    lax.fori_loop(0, n, body, 0)


def _wait_row_gather(src_hbm, dst_ref, sem, n):
    pltpu.make_async_copy(src_hbm.at[pl.ds(0, n)], dst_ref.at[pl.ds(0, n)], sem).wait()


def _pipelined_row_gather(idx_ref, idx_next_ref, src_hbm, buf_ref, sem_ref, n):
    i = pl.program_id(0)
    slot = i % 2

    @pl.when(i == 0)
    def _():
        _start_row_gather(idx_ref, src_hbm, buf_ref.at[0], sem_ref.at[0], n)

    @pl.when(i + 1 < pl.num_programs(0))
    def _():
        _start_row_gather(idx_next_ref, src_hbm, buf_ref.at[1 - slot], sem_ref.at[1 - slot], n)

    _wait_row_gather(src_hbm, buf_ref.at[slot], sem_ref.at[slot], n)
    return slot


def _gather_norm_kernel(idx_ref, idx_next_ref, x_hbm, g_ref, o_ref, buf_ref, sem_ref):
    rows = o_ref.shape[0]
    slot = _pipelined_row_gather(idx_ref, idx_next_ref, x_hbm, buf_ref, sem_ref, rows)

    def norm_rows(c, carry):
        r0 = pl.multiple_of(c * NORM_ROWS, NORM_ROWS)
        o_ref[pl.ds(r0, NORM_ROWS), :] = _rms(buf_ref[slot, pl.ds(r0, NORM_ROWS), :], g_ref[...]).astype(o_ref.dtype)
        return carry

    lax.fori_loop(0, rows // NORM_ROWS, norm_rows, 0)


def _row_index_specs(idx, rows):
    n = idx.shape[0] // rows
    idx3 = idx.reshape(n, 1, rows)
    cur = pl.BlockSpec((1, 1, rows), lambda i: (i, 0, 0), memory_space=pltpu.SMEM)
    nxt = pl.BlockSpec((1, 1, rows), lambda i: (jnp.minimum(i + 1, n - 1), 0, 0), memory_space=pltpu.SMEM)
    return idx3, cur, nxt


def moe_gather_norm(x, g, src):
    t, d = x.shape
    rows = 2 * MOE_GATHER_ROWS
    n_rows = src.shape[0]
    idx3, cur, nxt = _row_index_specs(src, rows)
    blocks = [_nbytes((rows, d), BF16)]
    scratch = 2 * _nbytes((rows, d), F32)
    return pl.pallas_call(
        _gather_norm_kernel,
        grid=(n_rows // rows,),
        in_specs=[cur, nxt, pl.BlockSpec(memory_space=pl.ANY), pl.BlockSpec((1, d), lambda i: (0, 0))],
        out_specs=pl.BlockSpec((rows, d), lambda i: (i, 0)),
        out_shape=jax.ShapeDtypeStruct((n_rows, d), BF16),
        scratch_shapes=[pltpu.VMEM((2, rows, d), F32), pltpu.SemaphoreType.DMA((2,))],
        compiler_params=_params(1, blocks, scratch + 3 * _nbytes((rows, d), F32)),
        name="moe_gather_norm",
    )(idx3, idx3, x, g.reshape(1, d))


def _expert_kernel(te_ref, tu_ref, h_ref, wrow_ref, wg_ref, wu_ref, wd_ref, o_ref, acc_ref):
    j = pl.program_id(1)

    @pl.when(j == 0)
    def _():
        acc_ref[...] = jnp.zeros_like(acc_ref)

    @pl.when(tu_ref[pl.program_id(0)] != 0)
    def _():
        h = h_ref[...]
        a = (jax.nn.silu(jnp.dot(h, wg_ref[...], preferred_element_type=F32))
             * jnp.dot(h, wu_ref[...], preferred_element_type=F32) * wrow_ref[...])
        acc_ref[...] += jnp.dot(a.astype(BF16), wd_ref[...], preferred_element_type=F32)

    @pl.when(j == pl.num_programs(1) - 1)
    def _():
        o_ref[...] = acc_ref[...]


def moe_experts(h_sorted, w_row, tile_expert, tile_used, w_gate, w_up, w_down, th=256):
    n_rows, d = h_sorted.shape
    fe = w_gate.shape[2]
    tm = MOE_TILE
    nj = fe // th

    def hidden(i, j, te, tu):
        return jnp.where(tu[i] != 0, j, nj - 1)

    blocks = [_nbytes((tm, d), BF16), _nbytes((tm, V7X_LANES), F32), 3 * _nbytes((d, th), BF16),
              _nbytes((tm, d), F32)]
    return pl.pallas_call(
        _expert_kernel,
        grid_spec=pltpu.PrefetchScalarGridSpec(
            num_scalar_prefetch=2,
            grid=(n_rows // tm, nj),
            in_specs=[pl.BlockSpec((tm, d), lambda i, j, te, tu: (i, 0)),
                      pl.BlockSpec((tm, 1), lambda i, j, te, tu: (i, 0)),
                      pl.BlockSpec((None, d, th), lambda i, j, te, tu: (te[i], 0, hidden(i, j, te, tu))),
                      pl.BlockSpec((None, d, th), lambda i, j, te, tu: (te[i], 0, hidden(i, j, te, tu))),
                      pl.BlockSpec((None, th, d), lambda i, j, te, tu: (te[i], hidden(i, j, te, tu), 0))],
            out_specs=pl.BlockSpec((tm, d), lambda i, j, te, tu: (i, 0)),
            scratch_shapes=[pltpu.VMEM((tm, d), F32)]),
        out_shape=jax.ShapeDtypeStruct((n_rows, d), F32),
        compiler_params=_params(2, blocks, 2 * _nbytes((tm, d), F32)),
        name="moe_experts",
    )(tile_expert, tile_used, h_sorted, w_row.reshape(n_rows, 1), w_gate, w_up, w_down)


def _combine_kernel(idx_ref, idx_next_ref, y_hbm, x_ref, g_ref, o_ref, h_ref, buf_ref, sem_ref):
    tc = x_ref.shape[0]
    slot = _pipelined_row_gather(idx_ref, idx_next_ref, y_hbm, buf_ref, sem_ref, 2 * tc)

    def add_rows(c, carry):
        r0 = pl.multiple_of(c * NORM_ROWS, NORM_ROWS)
        rs = pl.ds(r0, NORM_ROWS)
        y = x_ref[rs, :] + buf_ref[slot, rs, :] + buf_ref[slot, pl.ds(tc + r0, NORM_ROWS), :]
        o_ref[rs, :] = y
        h_ref[rs, :] = _rms(y, g_ref[...]).astype(h_ref.dtype)
        return carry

    lax.fori_loop(0, tc // NORM_ROWS, add_rows, 0)


def moe_combine(x, y_sorted, pos, g):
    t, d = x.shape
    tc = MOE_GATHER_ROWS
    n = t // tc
    idx = pos.reshape(n, tc, 2).transpose(0, 2, 1).reshape(n * 2 * tc)
    idx3, cur, nxt = _row_index_specs(idx, 2 * tc)
    blocks = [2 * _nbytes((tc, d), F32), _nbytes((tc, d), BF16)]
    scratch = 2 * _nbytes((2 * tc, d), F32)
    return pl.pallas_call(
        _combine_kernel,
        grid=(n,),
        in_specs=[cur, nxt, pl.BlockSpec(memory_space=pl.ANY), pl.BlockSpec((tc, d), lambda i: (i, 0)),
                  pl.BlockSpec((1, d), lambda i: (0, 0))],
        out_specs=[pl.BlockSpec((tc, d), lambda i: (i, 0)), pl.BlockSpec((tc, d), lambda i: (i, 0))],
        out_shape=[jax.ShapeDtypeStruct((t, d), F32), jax.ShapeDtypeStruct((t, d), BF16)],
        scratch_shapes=[pltpu.VMEM((2, 2 * tc, d), F32), pltpu.SemaphoreType.DMA((2,))],
        compiler_params=_params(1, blocks, scratch),
        name="moe_combine",
    )(idx3, idx3, y_sorted, x, g.reshape(1, d))


def _ple_kernel(p_ref, g1_ref, wp_ref, wg_ref, x_ref, gain_ref, o_ref, h_ref):
    proj = jnp.dot(p_ref[...], wp_ref[...], preferred_element_type=F32)
    gate = jnp.dot(g1_ref[...], wg_ref[...], preferred_element_type=F32)
    y = x_ref[...] + proj * jax.nn.sigmoid(gate)
    o_ref[...] = y
    h_ref[...] = _rms(y, gain_ref[...]).astype(h_ref.dtype)


def ple_up_norm(p_bf, g1, w_proj, w_gate_up, x, gain, h_dtype, tm=256):
    t, d = x.shape
    pd = p_bf.shape[1]
    blocks = [2 * _nbytes((tm, pd), BF16), 2 * _nbytes((pd, d), BF16), 2 * _nbytes((tm, d), F32),
              _nbytes((tm, d), h_dtype)]
    return pl.pallas_call(
        _ple_kernel,
        grid=(t // tm,),
        in_specs=[pl.BlockSpec((tm, pd), lambda i: (i, 0)),
                  pl.BlockSpec((tm, pd), lambda i: (i, 0)),
                  pl.BlockSpec((pd, d), lambda i: (0, 0)),
                  pl.BlockSpec((pd, d), lambda i: (0, 0)),
                  pl.BlockSpec((tm, d), lambda i: (i, 0)),
                  pl.BlockSpec((1, d), lambda i: (0, 0))],
        out_specs=[pl.BlockSpec((tm, d), lambda i: (i, 0)), pl.BlockSpec((tm, d), lambda i: (i, 0))],
        out_shape=[jax.ShapeDtypeStruct((t, d), F32), jax.ShapeDtypeStruct((t, d), h_dtype)],
        compiler_params=_params(1, blocks, 4 * _nbytes((tm, d), F32)),
        name="ple_up_norm",
    )(p_bf, g1, w_proj, w_gate_up, x, gain.reshape(1, d))


def _layer(x, h, p_bf, w, li, next_gain, h_dtype, *, batch, seq):
    t, d = x.shape
    cdim = w["conv_dw"].shape[2]
    kdim, vdim = GLA_HEADS * GLA_DK, GLA_HEADS * GLA_DV
    tm, tn, tn2 = 1024, 512, 256
    tnw = 1024
    fj = li // 2

    w_in = w["w_in"]
    o_q = 2 * cdim
    o_alr = o_q + 2 * kdim + 2 * vdim
    o_gate = o_alr + GLA_RANK
    c_og = 2 * kdim + vdim
    w_gate = w_in[li, :, o_gate:]
    w_alr = jnp.zeros((d, V7X_LANES), BF16).at[:, :GLA_RANK].set(w_in[li, :, o_alr:o_gate])
    u = fused_matmul([h], [(w_in, *_layer_col_tile(li, d, tn)), (w_in, *_layer_col_tile(li, d, tn, cdim // tn))],
                     [], _ep_glu, n_out=cdim, tm=tm, tn=tn, out_dtype=F32, name="in_glu", pair_a=(0, 0))
    z = fused_matmul([h], [(w_in, *_layer_col_tile(li, d, tnw, o_q // tnw))], [],
                     functools.partial(_ep_silu_from, silu_from=c_og // tnw),
                     n_out=o_alr - o_q, tm=tm, tn=tnw, out_dtype=BF16, name="in_z")
    gates = fused_matmul([h], [(w_gate, *_col_tile(d, tnw))], [], _ep_sigmoid,
                         n_out=2 * d, tm=tm, tn=tnw, out_dtype=BF16, name="in_gate")
    a_lr = fused_matmul([h], [(w_alr, *_col_tile(d, V7X_LANES))], [], _ep_plain,
                        n_out=V7X_LANES, tm=tm, tn=V7X_LANES, out_dtype=F32, name="in_alr")

    y_conv = conv_branch(u, w["conv_dw"][li], w["conv_dw_b"][li], w["conv_ln_g"][li], w["conv_ln_b"][li], seq=seq)
    y_gla = gla_branch(z, a_lr, w["gla_a_up"][li], w["gla_a_b"][li], w["gla_onorm"][li], batch=batch, seq=seq,
                       q_col=0, k_col=kdim, v_col=2 * kdim, og_col=c_og)

    def tile(off=0):
        return (tm, tn), lambda i, j: (i, j + off // tn)

    mix = fused_matmul([y_conv, y_gla],
                       [(w["conv_w_out"], *_layer_col_tile(li, cdim, tn)),
                        (w["gla_w_out"], *_layer_col_tile(li, vdim, tn))],
                       [(gates, *tile()), (gates, *tile(d))],
                       _ep_gated_sum, n_out=d, tm=tm, tn=tn, out_dtype=BF16, name="branch_out", w_resident=True)
    x = fused_matmul([mix], [(w["w_o"], *_layer_col_tile(li, d, tn))], [(x, *tile())],
                     _ep_residual, n_out=d, tm=tm, tn=tn, out_dtype=F32, name="w_o", w_resident=True)

    if li % 2 == 0:
        h = rmsnorm(x, w["norm_ffn"][li], BF16)
        f = w["ffn_w_gate"].shape[2]
        act = fused_matmul([h], [(w["ffn_w_gate"], *_layer_col_tile(fj, d, tn2)),
                                 (w["ffn_w_up"], *_layer_col_tile(fj, d, tn2))], [],
                           _ep_swiglu, n_out=f, tm=tm, tn=tn2, out_dtype=BF16, name="ffn_up", pair_a=(0, 0),
                           w_resident=True)
        x = fused_matmul([act], [(w["ffn_w_down"][fj].astype(BF16), *_col_tile(f, tn))], [(x, *tile())],
                         _ep_residual, n_out=d, tm=tm, tn=tn, out_dtype=F32, name="ffn_down")
    else:
        sel, wts = moe_route(x, w["norm_ffn"][li], w["router_w"][fj], w["router_b"][fj])
        src, w_row, pos, tile_expert, tile_used = _moe_plan(sel, wts)
        h_sorted = moe_gather_norm(x, w["norm_ffn"][li], src)
        y_sorted = moe_experts(h_sorted, w_row, tile_expert, tile_used, w["moe_w_gate"][fj].astype(BF16),
                               w["moe_w_up"][fj].astype(BF16), w["moe_w_down"][fj].astype(BF16))
        x, h = moe_combine(x, y_sorted, pos, w["ple_norm"][li])

    if li % 2 == 0:
        h = rmsnorm(x, w["ple_norm"][li], BF16)
    pd = w["ple_gate_down"].shape[2]
    g1 = fused_matmul([h], [(w["ple_gate_down"][li].astype(BF16), *_col_tile(d, pd))], [], _ep_plain,
                      n_out=pd, tm=tm, tn=pd, out_dtype=BF16, name="ple_down")
    return ple_up_norm(p_bf, g1, w["ple_proj"][li].astype(BF16), w["ple_gate_up"][li].astype(BF16), x,
                       next_gain, h_dtype)


def kernel(x, p, norm_mix, w_in, conv_dw, conv_dw_b, conv_ln_g, conv_ln_b, conv_w_out, gla_a_up, gla_a_b, gla_onorm, gla_w_out, w_o, norm_ffn, ffn_w_gate, ffn_w_up, ffn_w_down, router_w, router_b, moe_w_gate, moe_w_up, moe_w_down, ple_norm, ple_gate_down, ple_gate_up, ple_proj, final_norm):
    batch, seq, d = x.shape
    w = dict(norm_mix=norm_mix, w_in=w_in.astype(BF16), conv_dw=conv_dw, conv_dw_b=conv_dw_b, conv_ln_g=conv_ln_g,
             conv_ln_b=conv_ln_b, conv_w_out=conv_w_out, gla_a_up=gla_a_up, gla_a_b=gla_a_b, gla_onorm=gla_onorm,
             gla_w_out=gla_w_out, w_o=w_o, norm_ffn=norm_ffn, ffn_w_gate=ffn_w_gate, ffn_w_up=ffn_w_up,
             ffn_w_down=ffn_w_down, router_w=router_w, router_b=router_b, moe_w_gate=moe_w_gate,
             moe_w_up=moe_w_up, moe_w_down=moe_w_down, ple_norm=ple_norm, ple_gate_down=ple_gate_down,
             ple_gate_up=ple_gate_up, ple_proj=ple_proj)
    depth = w_in.shape[0]
    xt = x.reshape(batch * seq, d)
    h = rmsnorm(xt, norm_mix[0], BF16)
    for li in range(depth):
        last = li == depth - 1
        p_bf = p[li].reshape(batch * seq, -1).astype(BF16)
        xt, h = _layer(xt, h, p_bf, w, li, final_norm if last else norm_mix[li + 1], F32 if last else BF16,
                       batch=batch, seq=seq)
    return h.reshape(batch, seq, d)
```
